```python
import math
import jax, jax.numpy as jnp
from jax import lax
import numpy as np


D_MODEL = 1024
BATCH = 16
SEQ = 4096
DEPTH = 4

N_META = 16
D_SSM = D_MODEL // 2
SSM_GROUP = 16
SSM_GROUPS = D_SSM // SSM_GROUP
SSM_STATE = 64
DT_MIN = 1e-3
DT_MAX = 1e-1
D_CONV = D_MODEL // 2
CONV_WIDTH = 31
N_HEADS = 8
HEAD_DIM = 64
D_ATTN = N_HEADS * HEAD_DIM
Q_BLOCK = 128
D_FF = 2816
N_BRANCH = 3
SPLITS = [D_SSM,
          D_SSM + 2 * D_CONV,
          D_SSM + 2 * D_CONV + D_ATTN,
          D_SSM + 2 * D_CONV + 2 * D_ATTN,
          D_SSM + 2 * D_CONV + 3 * D_ATTN]
D_IN = D_SSM + 2 * D_CONV + 3 * D_ATTN + N_BRANCH * D_MODEL
RMS_EPS = 1e-6
LN_EPS = 1e-5
F32 = jnp.float32

kernel_name = 'hybrid_s5_conformer_stickbreaking_block'


def rms_norm(x, g):
    xf = x.astype(F32)
    y = xf * lax.rsqrt(jnp.mean(xf * xf, axis=-1, keepdims=True) + RMS_EPS)
    return (y * g.astype(F32)).astype(x.dtype)


def layer_norm(x, g, b):
    xf = x.astype(F32)
    mu = jnp.mean(xf, axis=-1, keepdims=True)
    xc = xf - mu
    y = xc * lax.rsqrt(jnp.mean(xc * xc, axis=-1, keepdims=True) + LN_EPS)
    return (y * g.astype(F32) + b.astype(F32)).astype(x.dtype)


def swiglu_ffn(x, w13, w2):
    a, b = jnp.split(x @ w13, 2, axis=-1)
    return (jax.nn.silu(a) * b) @ w2


def _complex_affine_combine(e1, e2):
    a1r, a1i, b1r, b1i = e1
    a2r, a2i, b2r, b2i = e2
    ar = a2r * a1r - a2i * a1i
    ai = a2r * a1i + a2i * a1r
    br = a2r * b1r - a2i * b1i + b2r
    bi = a2r * b1i + a2i * b1r + b2i
    return (ar, ai, br, bi)


def s5_mixer(u, lam_re, lam_im, log_dt, b_re, b_im, c_re, c_im, d_skip, w_glu):
    Bsz, L, _ = u.shape
    uf = u.astype(F32).reshape(Bsz, L, SSM_GROUPS, SSM_GROUP)
    dt = jnp.exp(log_dt.astype(F32))[:, None]
    lr = lam_re.astype(F32)
    li = lam_im.astype(F32)
    mag = jnp.exp(lr * dt)
    ab_re = mag * jnp.cos(li * dt)
    ab_im = mag * jnp.sin(li * dt)
    den = lr * lr + li * li
    nr = ab_re - 1.0
    ni = ab_im
    coef_re = (nr * lr + ni * li) / den
    coef_im = (ni * lr - nr * li) / den
    br = b_re.astype(F32)
    bi = b_im.astype(F32)
    bb_re = coef_re[..., None] * br - coef_im[..., None] * bi
    bb_im = coef_re[..., None] * bi + coef_im[..., None] * br
    bu_re = jnp.einsum('gnc,blgc->lbgn', bb_re, uf)
    bu_im = jnp.einsum('gnc,blgc->lbgn', bb_im, uf)
    a_re = jnp.broadcast_to(ab_re, (L, 1, SSM_GROUPS, SSM_STATE))
    a_im = jnp.broadcast_to(ab_im, (L, 1, SSM_GROUPS, SSM_STATE))
    _, _, s_re, s_im = lax.associative_scan(_complex_affine_combine,
                                            (a_re, a_im, bu_re, bu_im), axis=0)
    y = (jnp.einsum('gcn,lbgn->blgc', c_re.astype(F32), s_re)
         - jnp.einsum('gcn,lbgn->blgc', c_im.astype(F32), s_im))
    y = y.reshape(Bsz, L, D_SSM) + d_skip.astype(F32) * u.astype(F32)
    y = jax.nn.gelu(y).astype(u.dtype)
    a, g = jnp.split(y @ w_glu, 2, axis=-1)
    return a * jax.nn.sigmoid(g)


def conformer_conv(xc, conv_w, conv_b, ln_g, ln_b, w_pw):
    a, g = jnp.split(xc, 2, axis=-1)
    h = a * jax.nn.sigmoid(g)
    h = lax.conv_general_dilated(
        h, conv_w[:, None, :].astype(h.dtype), window_strides=(1,),
        padding=((CONV_WIDTH - 1, 0),),
        dimension_numbers=('NWC', 'WIO', 'NWC'),
        feature_group_count=D_CONV) + conv_b
    h = jax.nn.silu(layer_norm(h, ln_g, ln_b))
    return h @ w_pw


def stick_breaking_attention(q, k, v, w_o):
    Bsz, L = q.shape[0], q.shape[1]
    scale = 1.0 / math.sqrt(HEAD_DIM)
    qf = q.astype(F32).transpose(0, 2, 1, 3)
    kf = k.astype(F32).transpose(0, 2, 1, 3)
    vf = v.astype(F32).transpose(0, 2, 1, 3)
    key_pos = jnp.arange(L)

    def attend(q_blk, q_pos):
        z = jnp.einsum('bhqd,bhkd->bhqk', q_blk, kf) * scale
        mask = key_pos[None, :] < q_pos[:, None]
        log_keep = jnp.where(mask, jax.nn.log_sigmoid(-z), 0.0)
        later = lax.cumsum(log_keep, axis=3, reverse=True) - log_keep
        w = jnp.where(mask, jnp.exp(jax.nn.log_sigmoid(z) + later), 0.0)
        return jnp.einsum('bhqk,bhkd->bhqd', w, vf)

    meta_out = attend(qf[:, :, :N_META], jnp.arange(N_META))
    n_blk = (L - N_META) // Q_BLOCK
    q_real = qf[:, :, N_META:].reshape(Bsz, N_HEADS, n_blk, Q_BLOCK, HEAD_DIM)
    q_real = q_real.transpose(2, 0, 1, 3, 4)
    pos = N_META + jnp.arange(n_blk * Q_BLOCK).reshape(n_blk, Q_BLOCK)
    real_out = lax.map(lambda a: attend(a[0], a[1]), (q_real, pos))
    real_out = real_out.transpose(1, 2, 0, 3, 4).reshape(Bsz, N_HEADS, n_blk * Q_BLOCK, HEAD_DIM)
    o = jnp.concatenate([meta_out, real_out], axis=2)
    o = o.transpose(0, 2, 1, 3).reshape(Bsz, L, D_ATTN).astype(q.dtype)
    return o @ w_o


def hybrid_mixer(xn, w_in, lam_re, lam_im, log_dt, b_re, b_im, c_re, c_im, d_skip, w_glu,
                 conv_w, conv_b, conv_ln_g, conv_ln_b, conv_w_out, attn_w_o, w_out):
    Bsz, L, _ = xn.shape
    proj = xn @ w_in
    u, xc, q, k, v, gates = jnp.split(proj, SPLITS, axis=-1)
    o_ssm = s5_mixer(u, lam_re, lam_im, log_dt, b_re, b_im, c_re, c_im, d_skip, w_glu)
    o_conv = conformer_conv(xc, conv_w, conv_b, conv_ln_g, conv_ln_b, conv_w_out)
    o_attn = stick_breaking_attention(q.reshape(Bsz, L, N_HEADS, HEAD_DIM),
                                      k.reshape(Bsz, L, N_HEADS, HEAD_DIM),
                                      v.reshape(Bsz, L, N_HEADS, HEAD_DIM), attn_w_o)
    g = jax.nn.sigmoid(gates.astype(F32)).reshape(Bsz, L, N_BRANCH, D_MODEL)
    merged = (g[:, :, 0] * o_ssm.astype(F32) + g[:, :, 1] * o_conv.astype(F32)
              + g[:, :, 2] * o_attn.astype(F32))
    return merged.astype(xn.dtype) @ w_out


def setup_inputs(seed: int = 0) -> dict:
    key = jax.random.key(seed)
    ks = jax.random.split(key, 32)

    def nrm(k, shape, scale):
        return scale * jax.random.normal(k, shape, F32)

    G, N = SSM_GROUPS, SSM_STATE
    lam_im_init = math.pi * jnp.arange(N, dtype=F32)
    return {
        'x': nrm(ks[0], (BATCH, SEQ, D_MODEL), 1.0),
        'meta_tokens': nrm(ks[1], (N_META, D_MODEL), 1.0),
        'ffn1_norm': 1.0 + nrm(ks[2], (DEPTH, D_MODEL), 0.05),
        'ffn1_w13': nrm(ks[3], (DEPTH, D_MODEL, 2 * D_FF), D_MODEL ** -0.5),
        'ffn1_w2': nrm(ks[4], (DEPTH, D_FF, D_MODEL), D_FF ** -0.5),
        'mix_norm': 1.0 + nrm(ks[5], (DEPTH, D_MODEL), 0.05),
        'w_in': nrm(ks[6], (DEPTH, D_MODEL, D_IN), D_MODEL ** -0.5),
        'ssm_lam_re': -0.5 + nrm(ks[7], (DEPTH, G, N), 0.01),
        'ssm_lam_im': lam_im_init + nrm(ks[8], (DEPTH, G, N), 0.01),
        'ssm_log_dt': jax.random.uniform(ks[9], (DEPTH, G), F32,
                                         math.log(DT_MIN), math.log(DT_MAX)),
        'ssm_b_re': nrm(ks[10], (DEPTH, G, N, SSM_GROUP), (2 * SSM_GROUP) ** -0.5),
        'ssm_b_im': nrm(ks[11], (DEPTH, G, N, SSM_GROUP), (2 * SSM_GROUP) ** -0.5),
        'ssm_c_re': nrm(ks[12], (DEPTH, G, SSM_GROUP, N), (2 * N) ** -0.5),
        'ssm_c_im': nrm(ks[13], (DEPTH, G, SSM_GROUP, N), (2 * N) ** -0.5),
        'ssm_d': nrm(ks[14], (DEPTH, D_SSM), 1.0),
        'ssm_w_glu': nrm(ks[15], (DEPTH, D_SSM, 2 * D_MODEL), D_SSM ** -0.5),
        'conv_w': nrm(ks[16], (DEPTH, CONV_WIDTH, D_CONV), CONV_WIDTH ** -0.5),
        'conv_b': nrm(ks[17], (DEPTH, D_CONV), 0.01),
        'conv_ln_g': 1.0 + nrm(ks[18], (DEPTH, D_CONV), 0.05),
        'conv_ln_b': nrm(ks[19], (DEPTH, D_CONV), 0.01),
        'conv_w_out': nrm(ks[20], (DEPTH, D_CONV, D_MODEL), D_CONV ** -0.5),
        'attn_w_o': nrm(ks[21], (DEPTH, D_ATTN, D_MODEL), D_ATTN ** -0.5),
        'w_out': nrm(ks[22], (DEPTH, D_MODEL, D_MODEL), D_MODEL ** -0.5),
        'ffn2_norm': 1.0 + nrm(ks[23], (DEPTH, D_MODEL), 0.05),
        'ffn2_w13': nrm(ks[24], (DEPTH, D_MODEL, 2 * D_FF), D_MODEL ** -0.5),
        'ffn2_w2': nrm(ks[25], (DEPTH, D_FF, D_MODEL), D_FF ** -0.5),
        'final_norm': 1.0 + nrm(ks[26], (D_MODEL,), 0.05),
    }


def reference(x, meta_tokens, ffn1_norm, ffn1_w13, ffn1_w2, mix_norm, w_in,
              ssm_lam_re, ssm_lam_im, ssm_log_dt, ssm_b_re, ssm_b_im, ssm_c_re, ssm_c_im,
              ssm_d, ssm_w_glu, conv_w, conv_b, conv_ln_g, conv_ln_b, conv_w_out,
              attn_w_o, w_out, ffn2_norm, ffn2_w13, ffn2_w2, final_norm):
    Bsz = x.shape[0]
    meta = jnp.broadcast_to(meta_tokens[None].astype(x.dtype), (Bsz, N_META, D_MODEL))
    h = jnp.concatenate([meta, x], axis=1)
    for i in range(DEPTH):
        h = h + 0.5 * swiglu_ffn(rms_norm(h, ffn1_norm[i]), ffn1_w13[i], ffn1_w2[i])
        h = h + hybrid_mixer(rms_norm(h, mix_norm[i]), w_in[i],
                             ssm_lam_re[i], ssm_lam_im[i], ssm_log_dt[i],
                             ssm_b_re[i], ssm_b_im[i], ssm_c_re[i], ssm_c_im[i],
                             ssm_d[i], ssm_w_glu[i],
                             conv_w[i], conv_b[i], conv_ln_g[i], conv_ln_b[i], conv_w_out[i],
                             attn_w_o[i], w_out[i])
        h = h + 0.5 * swiglu_ffn(rms_norm(h, ffn2_norm[i]), ffn2_w13[i], ffn2_w2[i])
    return rms_norm(h, final_norm)[:, N_META:]
```

```python
import functools
import math

import jax
import jax.numpy as jnp
from jax import lax
from jax.experimental import pallas as pl
from jax.experimental.pallas import tpu as pltpu

F32 = jnp.float32
BF16 = jnp.bfloat16

N_META = 16
SSM_GROUP = 16
SSM_STATE = 64
CONV_WIDTH = 31
N_HEADS = 8
HEAD_DIM = 64
RMS_EPS = 1e-6
LN_EPS = 1e-5

LANES = 128
VMEM_LIMIT_BYTES = 56 * 1024 * 1024

TIME_TILE = 128
TOKEN_TILE = 512
SSM_TIME_CHUNK = 64
SSM_GROUP_BLOCK = 8
CONV_TIME_CHUNK = 64
CONV_HALO = 32
CONV_SUB = 8


def _params(semantics):
    return pltpu.CompilerParams(dimension_semantics=semantics,
                                vmem_limit_bytes=VMEM_LIMIT_BYTES)


def _resident(shape):
    zeros = (0,) * len(shape)
    return pl.BlockSpec(shape, lambda *_: zeros, pipeline_mode=pl.Buffered(1))


def _rms(x, g):
    return x * lax.rsqrt(jnp.mean(x * x, axis=-1, keepdims=True) + RMS_EPS) * g


def _sigmoid(x):
    return 1.0 / (1.0 + jnp.exp(-x))


def _ffn_kernel(h_ref, g_ref, w13_ref, w2_ref, o_ref, *, d_ff):
    x = h_ref[...]
    xb = _rms(x, g_ref[...]).astype(BF16)
    ab = jnp.dot(xb, w13_ref[...], preferred_element_type=F32)
    a = ab[:, :d_ff]
    b = ab[:, d_ff:]
    hm = (a * _sigmoid(a) * b).astype(BF16)
    y = jnp.dot(hm, w2_ref[...], preferred_element_type=F32)
    o_ref[...] = x + 0.5 * y


def _ffn(h, g, w13, w2):
    rows, d = h.shape
    d_ff = w2.shape[0]
    row_spec = pl.BlockSpec((TOKEN_TILE, d), lambda i: (i, 0))
    return pl.pallas_call(
        functools.partial(_ffn_kernel, d_ff=d_ff),
        grid=(rows // TOKEN_TILE,),
        in_specs=[row_spec, _resident((1, d)), _resident(w13.shape), _resident(w2.shape)],
        out_specs=row_spec,
        out_shape=jax.ShapeDtypeStruct((rows, d), F32),
        compiler_params=_params(("parallel",)),
        name="ffn",
    )(h, g, w13, w2)


def _proj_kernel(h_ref, g_ref, w_ref, u_ref, hg_ref, qkv_ref, *, d_ssm, d_conv, d_attn):
    xb = _rms(h_ref[...], g_ref[...]).astype(BF16)
    p = jnp.dot(xb, w_ref[...], preferred_element_type=F32)
    c0 = d_ssm
    c1 = c0 + d_conv
    c2 = c1 + d_conv
    c3 = c2 + d_attn
    u_ref[...] = p[:, :c0]
    hg_ref[...] = p[:, c0:c1] * _sigmoid(p[:, c1:c2])
    scale = 1.0 / math.sqrt(HEAD_DIM)
    qkv_ref[:, :d_attn] = (p[:, c2:c3] * scale).astype(BF16)
    qkv_ref[:, d_attn:] = p[:, c3:].astype(BF16)


def _proj(h, g, w, d_ssm, d_conv, d_attn):
    rows, d = h.shape

    def row_spec(width):
        return pl.BlockSpec((TOKEN_TILE, width), lambda i: (i, 0))

    return pl.pallas_call(
        functools.partial(_proj_kernel, d_ssm=d_ssm, d_conv=d_conv, d_attn=d_attn),
        grid=(rows // TOKEN_TILE,),
        in_specs=[row_spec(d), _resident((1, d)), _resident(w.shape)],
        out_specs=[row_spec(d_ssm), row_spec(d_conv), row_spec(3 * d_attn)],
        out_shape=[jax.ShapeDtypeStruct((rows, d_ssm), F32),
                   jax.ShapeDtypeStruct((rows, d_conv), F32),
                   jax.ShapeDtypeStruct((rows, 3 * d_attn), BF16)],
        compiler_params=_params(("parallel",)),
        name="mixer_in_proj",
    )(h, g, w)


def _merge_kernel(h_ref, g_ref, wg_ref, y_ref, wglu_ref, hc_ref, wpw_ref, oa_ref, wo_ref,
                  wout_ref, o_ref, *, d):
    x = h_ref[...]
    xb = _rms(x, g_ref[...]).astype(BF16)
    gates = _sigmoid(jnp.dot(xb, wg_ref[...], preferred_element_type=F32))
    sg = jnp.dot(y_ref[...], wglu_ref[...], preferred_element_type=F32)
    o_ssm = sg[:, :d] * _sigmoid(sg[:, d:])
    o_conv = jnp.dot(hc_ref[...], wpw_ref[...], preferred_element_type=F32)
    o_attn = jnp.dot(oa_ref[...], wo_ref[...], preferred_element_type=F32)
    merged = gates[:, :d] * o_ssm + gates[:, d:2 * d] * o_conv + gates[:, 2 * d:] * o_attn
    o_ref[...] = x + jnp.dot(merged.astype(BF16), wout_ref[...], preferred_element_type=F32)


def _merge(h, g, wg, y, wglu, hc, wpw, oa, wo, wout):
    rows, d = h.shape

    def row_spec(width):
        return pl.BlockSpec((TOKEN_TILE, width), lambda i: (i, 0))

    return pl.pallas_call(
        functools.partial(_merge_kernel, d=d),
        grid=(rows // TOKEN_TILE,),
        in_specs=[row_spec(d), _resident((1, d)), _resident(wg.shape),
                  row_spec(y.shape[1]), _resident(wglu.shape),
                  row_spec(hc.shape[1]), _resident(wpw.shape),
                  row_spec(oa.shape[1]), _resident(wo.shape),
                  _resident(wout.shape)],
        out_specs=row_spec(d),
        out_shape=jax.ShapeDtypeStruct((rows, d), F32),
        compiler_params=_params(("parallel",)),
        name="mixer_merge",
    )(h, g, wg, y, wglu, hc, wpw, oa, wo, wout)


def _final_norm_kernel(h_ref, g_ref, o_ref):
    o_ref[...] = _rms(h_ref[...], g_ref[...])


def _final_norm(h, g):
    rows, d = h.shape
    row_spec = pl.BlockSpec((TOKEN_TILE, d), lambda i: (i, 0))
    return pl.pallas_call(
        _final_norm_kernel,
        grid=(rows // TOKEN_TILE,),
        in_specs=[row_spec, _resident((1, d))],
        out_specs=row_spec,
        out_shape=jax.ShapeDtypeStruct((rows, d), F32),
        compiler_params=_params(("parallel",)),
        name="final_norm",
    )(h, g)


def _ssm_kernel(u_ref, lre_ref, lim_ref, ldt_ref, bre_ref, bim_ref, cre_ref, cim_ref, d_ref,
                y_ref, are_s, aim_s, bbre_s, bbim_s, hre_s, him_s, bure_s, buim_s, sre_s, sim_s,
                *, batch, steps):
    @pl.when(pl.program_id(1) == 0)
    def _discretise():
        dt = jnp.exp(ldt_ref[0])
        lr = lre_ref[0]
        li = lim_ref[0]
        mag = jnp.exp(lr * dt)
        ab_re = mag * jnp.cos(li * dt)
        ab_im = mag * jnp.sin(li * dt)
        den = lr * lr + li * li
        nr = ab_re - 1.0
        ni = ab_im
        coef_re = (nr * lr + ni * li) / den
        coef_im = (ni * lr - nr * li) / den
        are_s[...] = jnp.broadcast_to(ab_re, are_s.shape)
        aim_s[...] = jnp.broadcast_to(ab_im, aim_s.shape)
        br = bre_ref[0]
        bi = bim_ref[0]
        bbre_s[...] = (coef_re * br - coef_im * bi).astype(BF16)
        bbim_s[...] = (coef_re * bi + coef_im * br).astype(BF16)
        hre_s[...] = jnp.zeros_like(hre_s)
        him_s[...] = jnp.zeros_like(him_s)

    u = u_ref[...]
    ub = u.astype(BF16)
    bure_s[...] = jnp.dot(ub, bbre_s[...], preferred_element_type=F32)
    buim_s[...] = jnp.dot(ub, bbim_s[...], preferred_element_type=F32)

    a_re = are_s[...]
    a_im = aim_s[...]

    def step(i, carry):
        h_re, h_im = carry
        rows = pl.ds(pl.multiple_of(i * batch, batch), batch)
        n_re = a_re * h_re - a_im * h_im + bure_s[rows, :]
        n_im = a_re * h_im + a_im * h_re + buim_s[rows, :]
        sre_s[rows, :] = n_re
        sim_s[rows, :] = n_im
        return n_re, n_im

    h_re, h_im = lax.fori_loop(0, steps, step, (hre_s[...], him_s[...]), unroll=4)
    hre_s[...] = h_re
    him_s[...] = h_im

    y = (jnp.dot(sre_s[...].astype(BF16), cre_ref[0], preferred_element_type=F32)
         - jnp.dot(sim_s[...].astype(BF16), cim_ref[0], preferred_element_type=F32))
    y = y + d_ref[0] * u
    y_ref[...] = jax.nn.gelu(y).astype(BF16)


def _block_diag(blocks):
    nb, k, r, c = blocks.shape
    eye = jnp.eye(k, dtype=blocks.dtype)
    return jnp.einsum("bkrc,kj->bkrjc", blocks, eye).reshape(nb, k * r, k * c)


def _ssm(u, lam_re, lam_im, log_dt, b_re, b_im, c_re, c_im, d_skip, batch):
    rows, d_ssm = u.shape
    groups = lam_re.shape[0]
    nb = groups // SSM_GROUP_BLOCK
    chan = SSM_GROUP_BLOCK * SSM_GROUP
    width = SSM_GROUP_BLOCK * SSM_STATE
    chunk_rows = SSM_TIME_CHUNK * batch

    lre = lam_re.reshape(nb, 1, width)
    lim = lam_im.reshape(nb, 1, width)
    ldt = jnp.repeat(log_dt, SSM_STATE).reshape(nb, 1, width)
    bre = _block_diag(b_re.reshape(nb, SSM_GROUP_BLOCK, SSM_STATE, SSM_GROUP).transpose(0, 1, 3, 2))
    bim = _block_diag(b_im.reshape(nb, SSM_GROUP_BLOCK, SSM_STATE, SSM_GROUP).transpose(0, 1, 3, 2))
    cre = _block_diag(c_re.reshape(nb, SSM_GROUP_BLOCK, SSM_GROUP, SSM_STATE).transpose(0, 1, 3, 2))
    cim = _block_diag(c_im.reshape(nb, SSM_GROUP_BLOCK, SSM_GROUP, SSM_STATE).transpose(0, 1, 3, 2))
    cre = cre.astype(BF16)
    cim = cim.astype(BF16)
    dsk = d_skip.reshape(nb, 1, chan)

    def per_block(shape):
        return pl.BlockSpec((1,) + shape, lambda g, t: (g, 0, 0))

    io_spec = pl.BlockSpec((chunk_rows, chan), lambda g, t: (t, g))
    return pl.pallas_call(
        functools.partial(_ssm_kernel, batch=batch, steps=SSM_TIME_CHUNK),
        grid=(nb, rows // chunk_rows),
        in_specs=[io_spec, per_block((1, width)), per_block((1, width)), per_block((1, width)),
                  per_block((chan, width)), per_block((chan, width)),
                  per_block((width, chan)), per_block((width, chan)), per_block((1, chan))],
        out_specs=io_spec,
        out_shape=jax.ShapeDtypeStruct((rows, d_ssm), BF16),
        scratch_shapes=[pltpu.VMEM((batch, width), F32), pltpu.VMEM((batch, width), F32),
                        pltpu.VMEM((chan, width), BF16), pltpu.VMEM((chan, width), BF16),
                        pltpu.VMEM((batch, width), F32), pltpu.VMEM((batch, width), F32),
                        pltpu.VMEM((chunk_rows, width), F32), pltpu.VMEM((chunk_rows, width), F32),
                        pltpu.VMEM((chunk_rows, width), F32), pltpu.VMEM((chunk_rows, width), F32)],
        compiler_params=_params(("parallel", "arbitrary")),
        name="s5_scan",
    )(u, lre, lim, ldt, bre, bim, cre, cim, dsk)


def _conv_kernel(cur_ref, prev_ref, w_ref, b_ref, g_ref, beta_ref, o_ref, win_s, acc_s,
                 *, batch, channels):
    halo_rows = CONV_HALO * batch
    sub_rows = CONV_SUB * batch
    first = pl.program_id(0) == 0
    win_s[:halo_rows, :] = jnp.where(first, 0.0, prev_ref[...])
    win_s[halo_rows:, :] = cur_ref[...]
    lead = CONV_HALO - (CONV_WIDTH - 1)

    for lb in range(channels // LANES):
        lanes = slice(lb * LANES, (lb + 1) * LANES)
        w = w_ref[:, lanes]

        def sub_block(s, _, lanes=lanes, w=w):
            r0 = pl.multiple_of(s * sub_rows, sub_rows)
            acc = jnp.zeros((sub_rows, LANES), F32)
            for j in range(CONV_WIDTH):
                acc = acc + w[j:j + 1, :] * win_s[pl.ds(r0 + (lead + j) * batch, sub_rows), lanes]
            acc_s[pl.ds(r0, sub_rows), lanes] = acc
            return 0

        lax.fori_loop(0, CONV_TIME_CHUNK // CONV_SUB, sub_block, 0)

    c = acc_s[...] + b_ref[...]
    mu = jnp.mean(c, axis=-1, keepdims=True)
    xc = c - mu
    y = xc * lax.rsqrt(jnp.mean(xc * xc, axis=-1, keepdims=True) + LN_EPS)
    y = y * g_ref[...] + beta_ref[...]
    o_ref[...] = (y * _sigmoid(y)).astype(BF16)


def _conv(hg, w, b, ln_g, ln_b, batch):
    rows, channels = hg.shape
    chunk_rows = CONV_TIME_CHUNK * batch
    halo_rows = CONV_HALO * batch
    ratio = CONV_TIME_CHUNK // CONV_HALO
    return pl.pallas_call(
        functools.partial(_conv_kernel, batch=batch, channels=channels),
        grid=(rows // chunk_rows,),
        in_specs=[pl.BlockSpec((chunk_rows, channels), lambda t: (t, 0)),
                  pl.BlockSpec((halo_rows, channels), lambda t: (jnp.maximum(t * ratio - 1, 0), 0)),
                  _resident(w.shape), _resident((1, channels)), _resident((1, channels)),
                  _resident((1, channels))],
        out_specs=pl.BlockSpec((chunk_rows, channels), lambda t: (t, 0)),
        out_shape=jax.ShapeDtypeStruct((rows, channels), BF16),
        scratch_shapes=[pltpu.VMEM((halo_rows + chunk_rows, channels), F32),
                        pltpu.VMEM((chunk_rows, channels), F32)],
        compiler_params=_params(("parallel",)),
        name="conformer_conv",
    )(hg, hg, w, b.reshape(1, channels), ln_g.reshape(1, channels), ln_b.reshape(1, channels))


def _attn_kernel(q_ref, k_ref, v_ref, o_ref):
    blk = TIME_TILE
    qi = pl.program_id(1)
    row = lax.broadcasted_iota(jnp.int32, (blk, blk), 0)
    col = lax.broadcasted_iota(jnp.int32, (blk, blk), 1)
    strictly_before = col < row
    jj = lax.broadcasted_iota(jnp.int32, (blk, 2 * blk), 0)
    ss = lax.broadcasted_iota(jnp.int32, (blk, 2 * blk), 1)
    suffix = jnp.where((jj > ss) | (ss >= blk), 1.0, 0.0).astype(BF16)

    def block(qh, kj, vj, acc, run, masked):
        z = lax.dot_general(qh, kj, (((1,), (1,)), ((), ())), preferred_element_type=F32)
        log_keep = -(jnp.maximum(z, 0.0) + jnp.log1p(jnp.exp(-jnp.abs(z))))
        if masked:
            log_keep = jnp.where(strictly_before, log_keep, 0.0)
        hi = log_keep.astype(BF16)
        lo = (log_keep - hi.astype(F32)).astype(BF16)
        sums = (jnp.dot(hi, suffix, preferred_element_type=F32)
                + jnp.dot(lo, suffix, preferred_element_type=F32))
        w = jnp.exp(z + log_keep + sums[:, :blk] + run)
        if masked:
            w = jnp.where(strictly_before, w, 0.0)
        acc = acc + jnp.dot(w.astype(BF16), vj, preferred_element_type=F32)
        return acc, run + sums[:, blk:]

    outs = []
    for h in range(N_HEADS):
        lanes = slice(h * HEAD_DIM, (h + 1) * HEAD_DIM)
        qh = q_ref[0, :, lanes]
        q0 = pl.multiple_of(qi * blk, blk)
        acc, run = block(qh, k_ref[0, pl.ds(q0, blk), lanes], v_ref[0, pl.ds(q0, blk), lanes],
                         jnp.zeros((blk, HEAD_DIM), F32), jnp.zeros((blk, blk), F32), True)

        def earlier(n, carry, qh=qh, lanes=lanes):
            k0 = pl.multiple_of((qi - 1 - n) * blk, blk)
            return block(qh, k_ref[0, pl.ds(k0, blk), lanes], v_ref[0, pl.ds(k0, blk), lanes],
                         carry[0], carry[1], False)

        acc, run = lax.fori_loop(0, qi, earlier, (acc, run))
        outs.append(acc)
    o_ref[0] = jnp.concatenate(outs, axis=-1).astype(BF16)


def _attention(qkv):
    batch, length, width = qkv.shape
    d_attn = width // 3
    return pl.pallas_call(
        _attn_kernel,
        grid=(batch, length // TIME_TILE),
        in_specs=[pl.BlockSpec((1, TIME_TILE, d_attn), lambda b, i: (b, i, 0)),
                  pl.BlockSpec((1, length, d_attn), lambda b, i: (b, 0, 1)),
                  pl.BlockSpec((1, length, d_attn), lambda b, i: (b, 0, 2))],
        out_specs=pl.BlockSpec((1, TIME_TILE, d_attn), lambda b, i: (b, i, 0)),
        out_shape=jax.ShapeDtypeStruct((batch, length, d_attn), BF16),
        compiler_params=_params(("parallel", "parallel")),
        name="stick_breaking_attention",
    )(qkv, qkv, qkv)


def kernel(x, meta_tokens, ffn1_norm, ffn1_w13, ffn1_w2, mix_norm, w_in, ssm_lam_re, ssm_lam_im, ssm_log_dt, ssm_b_re, ssm_b_im, ssm_c_re, ssm_c_im, ssm_d, ssm_w_glu, conv_w, conv_b, conv_ln_g, conv_ln_b, conv_w_out, attn_w_o, w_out, ffn2_norm, ffn2_w13, ffn2_w2, final_norm):
    batch, seq, d = x.shape
    depth = w_in.shape[0]
    d_ssm = ssm_d.shape[1]
    d_conv = conv_w.shape[2]
    d_attn = attn_w_o.shape[1]
    n_mix = d_ssm + 2 * d_conv + 3 * d_attn
    assert d_attn == N_HEADS * HEAD_DIM and batch % 16 == 0 and d_ssm % (SSM_GROUP_BLOCK * SSM_GROUP) == 0
    length = seq + N_META
    padded = -(-length // TIME_TILE) * TIME_TILE
    rows = padded * batch
    assert rows % TOKEN_TILE == 0

    meta = jnp.broadcast_to(meta_tokens[:, None, :].astype(x.dtype), (N_META, batch, d))
    tail = jnp.zeros((padded - length, batch, d), x.dtype)
    h = jnp.concatenate([meta, jnp.swapaxes(x, 0, 1), tail], axis=0).reshape(rows, d)

    for i in range(depth):
        h = _ffn(h, ffn1_norm[i].reshape(1, d), ffn1_w13[i].astype(BF16), ffn1_w2[i].astype(BF16))

        g_mix = mix_norm[i].reshape(1, d)
        w_in_b = w_in[i].astype(BF16)
        u, hg, qkv = _proj(h, g_mix, w_in_b[:, :n_mix], d_ssm, d_conv, d_attn)
        y = _ssm(u, ssm_lam_re[i], ssm_lam_im[i], ssm_log_dt[i], ssm_b_re[i], ssm_b_im[i],
                 ssm_c_re[i], ssm_c_im[i], ssm_d[i], batch)
        hc = _conv(hg, conv_w[i], conv_b[i], conv_ln_g[i], conv_ln_b[i], batch)
        qkv_bm = jnp.swapaxes(qkv.reshape(padded, batch, 3 * d_attn), 0, 1)
        oa = jnp.swapaxes(_attention(qkv_bm), 0, 1).reshape(rows, d_attn)
        h = _merge(h, g_mix, w_in_b[:, n_mix:], y, ssm_w_glu[i].astype(BF16),
                   hc, conv_w_out[i].astype(BF16), oa, attn_w_o[i].astype(BF16),
                   w_out[i].astype(BF16))

        h = _ffn(h, ffn2_norm[i].reshape(1, d), ffn2_w13[i].astype(BF16), ffn2_w2[i].astype(BF16))

    out = _final_norm(h, final_norm.reshape(1, d)).reshape(padded, batch, d)
    return jnp.swapaxes(out[N_META:length], 0, 1)
```

```python
import functools
import math

import jax
import jax.numpy as jnp
from jax import lax
from jax.experimental import pallas as pl
from jax.experimental.pallas import tpu as pltpu

F32 = jnp.float32
BF16 = jnp.bfloat16

N_META = 16
SSM_GROUP = 16
SSM_STATE = 64
CONV_WIDTH = 31
N_HEADS = 8
HEAD_DIM = 64
RMS_EPS = 1e-6
LN_EPS = 1e-5

LANES = 128
VMEM_LIMIT_BYTES = 56 * 1024 * 1024

TIME_TILE = 128
TOKEN_TILE = 512
SSM_TIME_CHUNK = 64
SSM_GROUP_BLOCK = 8
CONV_TIME_CHUNK = 64
CONV_HALO = 32
CONV_SUB = 8
ATTN_HEADS_PER_GROUP = 4
ATTN_SKIP_DECAY = 100.0


def _params(semantics):
    return pltpu.CompilerParams(dimension_semantics=semantics,
                                vmem_limit_bytes=VMEM_LIMIT_BYTES)


def _resident(shape):
    zeros = (0,) * len(shape)
    return pl.BlockSpec(shape, lambda *_: zeros, pipeline_mode=pl.Buffered(1))


def _rms(x, g):
    return x * lax.rsqrt(jnp.mean(x * x, axis=-1, keepdims=True) + RMS_EPS) * g


def _sigmoid(x):
    return 1.0 / (1.0 + jnp.exp(-x))


def _ffn_kernel(h_ref, g_ref, w13_ref, w2_ref, o_ref, *, d_ff):
    x = h_ref[...]
    xb = _rms(x, g_ref[...]).astype(BF16)
    ab = jnp.dot(xb, w13_ref[...], preferred_element_type=F32)
    a = ab[:, :d_ff]
    b = ab[:, d_ff:]
    hm = (a * _sigmoid(a) * b).astype(BF16)
    y = jnp.dot(hm, w2_ref[...], preferred_element_type=F32)
    o_ref[...] = x + 0.5 * y


def _ffn(h, g, w13, w2):
    rows, d = h.shape
    d_ff = w2.shape[0]
    row_spec = pl.BlockSpec((TOKEN_TILE, d), lambda i: (i, 0))
    return pl.pallas_call(
        functools.partial(_ffn_kernel, d_ff=d_ff),
        grid=(rows // TOKEN_TILE,),
        in_specs=[row_spec, _resident((1, d)), _resident(w13.shape), _resident(w2.shape)],
        out_specs=row_spec,
        out_shape=jax.ShapeDtypeStruct((rows, d), F32),
        compiler_params=_params(("parallel",)),
        name="ffn",
    )(h, g, w13, w2)


def _proj_kernel(h_ref, g_ref, w_ref, u_ref, hg_ref, q_ref, k_ref, v_ref, *, d_ssm, d_conv, d_attn):
    xb = _rms(h_ref[...], g_ref[...]).astype(BF16)
    p = jnp.dot(xb, w_ref[...], preferred_element_type=F32)
    c0 = d_ssm
    c1 = c0 + d_conv
    c2 = c1 + d_conv
    c3 = c2 + d_attn
    c4 = c3 + d_attn
    u_ref[...] = p[:, :c0]
    hg_ref[...] = p[:, c0:c1] * _sigmoid(p[:, c1:c2])
    scale = 1.0 / math.sqrt(HEAD_DIM)
    q_ref[...] = (p[:, c2:c3] * scale).astype(BF16)
    k_ref[...] = p[:, c3:c4].astype(BF16)
    v_ref[...] = p[:, c4:].astype(BF16)


def _proj(h, g, w, d_ssm, d_conv, d_attn):
    rows, d = h.shape

    def row_spec(width):
        return pl.BlockSpec((TOKEN_TILE, width), lambda i: (i, 0))

    return pl.pallas_call(
        functools.partial(_proj_kernel, d_ssm=d_ssm, d_conv=d_conv, d_attn=d_attn),
        grid=(rows // TOKEN_TILE,),
        in_specs=[row_spec(d), _resident((1, d)), _resident(w.shape)],
        out_specs=[row_spec(d_ssm), row_spec(d_conv)] + [row_spec(d_attn)] * 3,
        out_shape=[jax.ShapeDtypeStruct((rows, d_ssm), F32),
                   jax.ShapeDtypeStruct((rows, d_conv), F32)]
                  + [jax.ShapeDtypeStruct((rows, d_attn), BF16)] * 3,
        compiler_params=_params(("parallel",)),
        name="mixer_in_proj",
    )(h, g, w)


def _merge_kernel(h_ref, g_ref, wg_ref, y_ref, wglu_ref, hc_ref, wpw_ref, oa_ref, wo_ref,
                  wout_ref, o_ref, *, d):
    x = h_ref[...]
    xb = _rms(x, g_ref[...]).astype(BF16)
    gates = _sigmoid(jnp.dot(xb, wg_ref[...], preferred_element_type=F32))
    sg = jnp.dot(y_ref[...], wglu_ref[...], preferred_element_type=F32)
    o_ssm = sg[:, :d] * _sigmoid(sg[:, d:])
    o_conv = jnp.dot(hc_ref[...], wpw_ref[...], preferred_element_type=F32)
    o_attn = jnp.dot(oa_ref[...], wo_ref[...], preferred_element_type=F32)
    merged = gates[:, :d] * o_ssm + gates[:, d:2 * d] * o_conv + gates[:, 2 * d:] * o_attn
    o_ref[...] = x + jnp.dot(merged.astype(BF16), wout_ref[...], preferred_element_type=F32)


def _merge(h, g, wg, y, wglu, hc, wpw, oa, wo, wout):
    rows, d = h.shape

    def row_spec(width):
        return pl.BlockSpec((TOKEN_TILE, width), lambda i: (i, 0))

    return pl.pallas_call(
        functools.partial(_merge_kernel, d=d),
        grid=(rows // TOKEN_TILE,),
        in_specs=[row_spec(d), _resident((1, d)), _resident(wg.shape),
                  row_spec(y.shape[1]), _resident(wglu.shape),
                  row_spec(hc.shape[1]), _resident(wpw.shape),
                  row_spec(oa.shape[1]), _resident(wo.shape),
                  _resident(wout.shape)],
        out_specs=row_spec(d),
        out_shape=jax.ShapeDtypeStruct((rows, d), F32),
        compiler_params=_params(("parallel",)),
        name="mixer_merge",
    )(h, g, wg, y, wglu, hc, wpw, oa, wo, wout)


def _final_norm_kernel(h_ref, g_ref, o_ref):
    o_ref[...] = _rms(h_ref[...], g_ref[...])


def _final_norm(h, g):
    rows, d = h.shape
    row_spec = pl.BlockSpec((TOKEN_TILE, d), lambda i: (i, 0))
    return pl.pallas_call(
        _final_norm_kernel,
        grid=(rows // TOKEN_TILE,),
        in_specs=[row_spec, _resident((1, d))],
        out_specs=row_spec,
        out_shape=jax.ShapeDtypeStruct((rows, d), F32),
        compiler_params=_params(("parallel",)),
        name="final_norm",
    )(h, g)


def _ssm_kernel(u_ref, lre_ref, lim_ref, ldt_ref, bre_ref, bim_ref, cre_ref, cim_ref, d_ref,
                y_ref, are_s, aim_s, bbre_s, bbim_s, hre_s, him_s, bure_s, buim_s, sre_s, sim_s,
                *, batch, steps):
    @pl.when(pl.program_id(1) == 0)
    def _discretise():
        dt = jnp.exp(ldt_ref[0])
        lr = lre_ref[0]
        li = lim_ref[0]
        mag = jnp.exp(lr * dt)
        ab_re = mag * jnp.cos(li * dt)
        ab_im = mag * jnp.sin(li * dt)
        den = lr * lr + li * li
        nr = ab_re - 1.0
        ni = ab_im
        coef_re = (nr * lr + ni * li) / den
        coef_im = (ni * lr - nr * li) / den
        are_s[...] = jnp.broadcast_to(ab_re, are_s.shape)
        aim_s[...] = jnp.broadcast_to(ab_im, aim_s.shape)
        br = bre_ref[0]
        bi = bim_ref[0]
        bbre_s[...] = (coef_re * br - coef_im * bi).astype(BF16)
        bbim_s[...] = (coef_re * bi + coef_im * br).astype(BF16)
        hre_s[...] = jnp.zeros_like(hre_s)
        him_s[...] = jnp.zeros_like(him_s)

    u = u_ref[...]
    ub = u.astype(BF16)
    bure_s[...] = jnp.dot(ub, bbre_s[...], preferred_element_type=F32)
    buim_s[...] = jnp.dot(ub, bbim_s[...], preferred_element_type=F32)

    a_re = are_s[...]
    a_im = aim_s[...]

    def step(i, carry):
        h_re, h_im = carry
        rows = pl.ds(pl.multiple_of(i * batch, batch), batch)
        n_re = a_re * h_re - a_im * h_im + bure_s[rows, :]
        n_im = a_re * h_im + a_im * h_re + buim_s[rows, :]
        sre_s[rows, :] = n_re
        sim_s[rows, :] = n_im
        return n_re, n_im

    h_re, h_im = lax.fori_loop(0, steps, step, (hre_s[...], him_s[...]), unroll=4)
    hre_s[...] = h_re
    him_s[...] = h_im

    y = (jnp.dot(sre_s[...].astype(BF16), cre_ref[0], preferred_element_type=F32)
         - jnp.dot(sim_s[...].astype(BF16), cim_ref[0], preferred_element_type=F32))
    y = y + d_ref[0] * u
    y_ref[...] = jax.nn.gelu(y).astype(BF16)


def _block_diag(blocks):
    nb, k, r, c = blocks.shape
    eye = jnp.eye(k, dtype=blocks.dtype)
    return jnp.einsum("bkrc,kj->bkrjc", blocks, eye).reshape(nb, k * r, k * c)


def _ssm(u, lam_re, lam_im, log_dt, b_re, b_im, c_re, c_im, d_skip, batch):
    rows, d_ssm = u.shape
    groups = lam_re.shape[0]
    nb = groups // SSM_GROUP_BLOCK
    chan = SSM_GROUP_BLOCK * SSM_GROUP
    width = SSM_GROUP_BLOCK * SSM_STATE
    chunk_rows = SSM_TIME_CHUNK * batch

    lre = lam_re.reshape(nb, 1, width)
    lim = lam_im.reshape(nb, 1, width)
    ldt = jnp.repeat(log_dt, SSM_STATE).reshape(nb, 1, width)
    bre = _block_diag(b_re.reshape(nb, SSM_GROUP_BLOCK, SSM_STATE, SSM_GROUP).transpose(0, 1, 3, 2))
    bim = _block_diag(b_im.reshape(nb, SSM_GROUP_BLOCK, SSM_STATE, SSM_GROUP).transpose(0, 1, 3, 2))
    cre = _block_diag(c_re.reshape(nb, SSM_GROUP_BLOCK, SSM_GROUP, SSM_STATE).transpose(0, 1, 3, 2))
    cim = _block_diag(c_im.reshape(nb, SSM_GROUP_BLOCK, SSM_GROUP, SSM_STATE).transpose(0, 1, 3, 2))
    cre = cre.astype(BF16)
    cim = cim.astype(BF16)
    dsk = d_skip.reshape(nb, 1, chan)

    def per_block(shape):
        return pl.BlockSpec((1,) + shape, lambda g, t: (g, 0, 0))

    io_spec = pl.BlockSpec((chunk_rows, chan), lambda g, t: (t, g))
    return pl.pallas_call(
        functools.partial(_ssm_kernel, batch=batch, steps=SSM_TIME_CHUNK),
        grid=(nb, rows // chunk_rows),
        in_specs=[io_spec, per_block((1, width)), per_block((1, width)), per_block((1, width)),
                  per_block((chan, width)), per_block((chan, width)),
                  per_block((width, chan)), per_block((width, chan)), per_block((1, chan))],
        out_specs=io_spec,
        out_shape=jax.ShapeDtypeStruct((rows, d_ssm), BF16),
        scratch_shapes=[pltpu.VMEM((batch, width), F32), pltpu.VMEM((batch, width), F32),
                        pltpu.VMEM((chan, width), BF16), pltpu.VMEM((chan, width), BF16),
                        pltpu.VMEM((batch, width), F32), pltpu.VMEM((batch, width), F32),
                        pltpu.VMEM((chunk_rows, width), F32), pltpu.VMEM((chunk_rows, width), F32),
                        pltpu.VMEM((chunk_rows, width), F32), pltpu.VMEM((chunk_rows, width), F32)],
        compiler_params=_params(("parallel", "arbitrary")),
        name="s5_scan",
    )(u, lre, lim, ldt, bre, bim, cre, cim, dsk)


def _conv_kernel(cur_ref, prev_ref, w_ref, b_ref, g_ref, beta_ref, o_ref, win_s, acc_s,
                 *, batch, channels):
    halo_rows = CONV_HALO * batch
    sub_rows = CONV_SUB * batch
    first = pl.program_id(0) == 0
    win_s[:halo_rows, :] = jnp.where(first, 0.0, prev_ref[...])
    win_s[halo_rows:, :] = cur_ref[...]
    lead = CONV_HALO - (CONV_WIDTH - 1)

    for lb in range(channels // LANES):
        lanes = slice(lb * LANES, (lb + 1) * LANES)
        w = w_ref[:, lanes]

        def sub_block(s, _, lanes=lanes, w=w):
            r0 = pl.multiple_of(s * sub_rows, sub_rows)
            acc = jnp.zeros((sub_rows, LANES), F32)
            for j in range(CONV_WIDTH):
                acc = acc + w[j:j + 1, :] * win_s[pl.ds(r0 + (lead + j) * batch, sub_rows), lanes]
            acc_s[pl.ds(r0, sub_rows), lanes] = acc
            return 0

        lax.fori_loop(0, CONV_TIME_CHUNK // CONV_SUB, sub_block, 0)

    c = acc_s[...] + b_ref[...]
    mu = jnp.mean(c, axis=-1, keepdims=True)
    xc = c - mu
    y = xc * lax.rsqrt(jnp.mean(xc * xc, axis=-1, keepdims=True) + LN_EPS)
    y = y * g_ref[...] + beta_ref[...]
    o_ref[...] = (y * _sigmoid(y)).astype(BF16)


def _conv(hg, w, b, ln_g, ln_b, batch):
    rows, channels = hg.shape
    chunk_rows = CONV_TIME_CHUNK * batch
    halo_rows = CONV_HALO * batch
    ratio = CONV_TIME_CHUNK // CONV_HALO
    return pl.pallas_call(
        functools.partial(_conv_kernel, batch=batch, channels=channels),
        grid=(rows // chunk_rows,),
        in_specs=[pl.BlockSpec((chunk_rows, channels), lambda t: (t, 0)),
                  pl.BlockSpec((halo_rows, channels), lambda t: (jnp.maximum(t * ratio - 1, 0), 0)),
                  _resident(w.shape), _resident((1, channels)), _resident((1, channels)),
                  _resident((1, channels))],
        out_specs=pl.BlockSpec((chunk_rows, channels), lambda t: (t, 0)),
        out_shape=jax.ShapeDtypeStruct((rows, channels), BF16),
        scratch_shapes=[pltpu.VMEM((halo_rows + chunk_rows, channels), F32),
                        pltpu.VMEM((chunk_rows, channels), F32)],
        compiler_params=_params(("parallel",)),
        name="conformer_conv",
    )(hg, hg, w, b.reshape(1, channels), ln_g.reshape(1, channels), ln_b.reshape(1, channels))


def _attn_kernel(qt_ref, k_ref, vt_ref, ot_ref):
    blk = TIME_TILE
    sub = 8
    n_sub = blk // sub
    hpg = ATTN_HEADS_PER_GROUP
    gw = hpg * HEAD_DIM
    cols = hpg * blk
    qi = pl.program_id(1)

    row = lax.broadcasted_iota(jnp.int32, (blk, cols), 0)
    key = (row & (sub - 1)) * n_sub + row // sub
    qry = lax.broadcasted_iota(jnp.int32, (blk, cols), 1) & (blk - 1)
    strictly_before = key < qry
    sublane = lax.broadcasted_iota(jnp.int32, (sub, cols), 0)

    chan_head = lax.broadcasted_iota(jnp.int32, (gw, blk), 0) // HEAD_DIM
    q_groups = []
    for g in range(N_HEADS // hpg):
        qg = qt_ref[0, 0, g * gw:(g + 1) * gw, :]
        zero = jnp.zeros_like(qg)
        q_groups.append(jnp.concatenate(
            [jnp.where(chan_head == hl, qg, zero) for hl in range(hpg)], axis=1))

    def softplus_suffix(sp, run):
        x = sp.reshape(n_sub, sub, cols)
        same_sublane = [None] * n_sub
        tot = jnp.zeros((sub, cols), F32)
        for g in reversed(range(n_sub)):
            same_sublane[g] = tot
            tot = tot + x[g]
        s = tot
        for shift in (1, 2, 4):
            moved = pltpu.roll(s, sub - shift, axis=0)
            s = s + jnp.where(sublane < sub - shift, moved, 0.0)
        higher = s - tot + run
        later = jnp.stack([part + higher for part in same_sublane], axis=0)
        return later.reshape(blk, cols), run + s[0:1, :]

    def scores(j):
        k0 = pl.multiple_of(j * blk, blk)
        return [jnp.dot(k_ref[0, pl.ds(k0, blk), g * gw:(g + 1) * gw], q_groups[g],
                        preferred_element_type=F32) for g in range(N_HEADS // hpg)]

    def block(j, zts, accs, runs, masked):
        new_accs, new_runs = [], []
        for g in range(N_HEADS // hpg):
            zt = zts[g]
            sp = jnp.maximum(zt, 0.0) + jnp.log(1.0 + jnp.exp(-jnp.abs(zt)))
            if masked:
                sp = jnp.where(strictly_before, sp, 0.0)
            later, run = softplus_suffix(sp, runs[g])
            w = jnp.exp(zt - sp - later)
            if masked:
                w = jnp.where(strictly_before, w, 0.0)
            w = w.astype(BF16)
            new_runs.append(run)
            for p in range(hpg // 2):
                c0 = 2 * p * blk
                h0 = g * hpg + 2 * p
                res = jnp.dot(vt_ref[0, j, h0 * HEAD_DIM:(h0 + 2) * HEAD_DIM, :], w[:, c0:c0 + 2 * blk],
                              preferred_element_type=F32)
                new_accs.append(accs[h0] + res[:HEAD_DIM, :blk])
                new_accs.append(accs[h0 + 1] + res[HEAD_DIM:, blk:])
        return new_accs, new_runs

    def most_decay(runs):
        m = runs[0]
        for run in runs[1:]:
            m = jnp.minimum(m, run)
        return jnp.min(m)

    accs = [jnp.zeros((HEAD_DIM, blk), F32)] * N_HEADS
    runs = [jnp.zeros((1, cols), F32)] * (N_HEADS // hpg)
    zts = scores(qi)
    nxt = scores(jnp.maximum(qi - 1, 0))
    accs, runs = block(qi, zts, accs, runs, True)

    def more(carry):
        n, m = carry[:2]
        return jnp.logical_and(n < qi, m < ATTN_SKIP_DECAY)

    def earlier(carry):
        n, _, z, a, r = carry
        z_next = scores(jnp.maximum(qi - 2 - n, 0))
        a, r = block(qi - 1 - n, z, a, r, False)
        return n + 1, most_decay(r), z_next, a, r

    accs = lax.while_loop(more, earlier, (jnp.int32(0), most_decay(runs), nxt, accs, runs))[3]
    ot_ref[0, 0] = jnp.concatenate(accs, axis=0).astype(BF16)


def _attention(qt, k, vt):
    batch, n_blk, d_attn, _ = qt.shape
    length = k.shape[1]
    tile_spec = pl.BlockSpec((1, 1, d_attn, TIME_TILE), lambda b, i: (b, i, 0, 0))
    return pl.pallas_call(
        _attn_kernel,
        grid=(batch, n_blk),
        in_specs=[tile_spec,
                  pl.BlockSpec((1, length, d_attn), lambda b, i: (b, 0, 0)),
                  pl.BlockSpec((1, n_blk, d_attn, TIME_TILE), lambda b, i: (b, 0, 0, 0))],
        out_specs=tile_spec,
        out_shape=jax.ShapeDtypeStruct((batch, n_blk, d_attn, TIME_TILE), BF16),
        compiler_params=_params(("parallel", "parallel")),
        name="stick_breaking_attention",
    )(qt, k, vt)


def kernel(x, meta_tokens, ffn1_norm, ffn1_w13, ffn1_w2, mix_norm, w_in, ssm_lam_re, ssm_lam_im, ssm_log_dt, ssm_b_re, ssm_b_im, ssm_c_re, ssm_c_im, ssm_d, ssm_w_glu, conv_w, conv_b, conv_ln_g, conv_ln_b, conv_w_out, attn_w_o, w_out, ffn2_norm, ffn2_w13, ffn2_w2, final_norm):
    batch, seq, d = x.shape
    depth = w_in.shape[0]
    d_ssm = ssm_d.shape[1]
    d_conv = conv_w.shape[2]
    d_attn = attn_w_o.shape[1]
    n_mix = d_ssm + 2 * d_conv + 3 * d_attn
    assert d_attn == N_HEADS * HEAD_DIM and batch % 16 == 0 and d_ssm % (SSM_GROUP_BLOCK * SSM_GROUP) == 0
    length = seq + N_META
    padded = -(-length // TIME_TILE) * TIME_TILE
    n_blk = padded // TIME_TILE
    rows = padded * batch
    assert rows % TOKEN_TILE == 0

    meta = jnp.broadcast_to(meta_tokens[:, None, :].astype(x.dtype), (N_META, batch, d))
    tail = jnp.zeros((padded - length, batch, d), x.dtype)
    h = jnp.concatenate([meta, jnp.swapaxes(x, 0, 1), tail], axis=0).reshape(rows, d)

    for i in range(depth):
        h = _ffn(h, ffn1_norm[i].reshape(1, d), ffn1_w13[i].astype(BF16), ffn1_w2[i].astype(BF16))

        g_mix = mix_norm[i].reshape(1, d)
        w_in_b = w_in[i].astype(BF16)
        u, hg, q, k, v = _proj(h, g_mix, w_in_b[:, :n_mix], d_ssm, d_conv, d_attn)
        y = _ssm(u, ssm_lam_re[i], ssm_lam_im[i], ssm_log_dt[i], ssm_b_re[i], ssm_b_im[i],
                 ssm_c_re[i], ssm_c_im[i], ssm_d[i], batch)
        hc = _conv(hg, conv_w[i], conv_b[i], conv_ln_g[i], conv_ln_b[i], batch)
        n_sub = TIME_TILE // 8
        blocked = (n_blk, TIME_TILE, batch, d_attn)
        keyed = (n_blk, 8, n_sub, batch, d_attn)
        qt = q.reshape(blocked).transpose(2, 0, 3, 1)
        vt = v.reshape(keyed).transpose(3, 0, 4, 2, 1).reshape(batch, n_blk, d_attn, TIME_TILE)
        kb = k.reshape(keyed).transpose(3, 0, 2, 1, 4).reshape(batch, padded, d_attn)
        oa = _attention(qt, kb, vt).transpose(1, 3, 0, 2).reshape(rows, d_attn)
        h = _merge(h, g_mix, w_in_b[:, n_mix:], y, ssm_w_glu[i].astype(BF16),
                   hc, conv_w_out[i].astype(BF16), oa, attn_w_o[i].astype(BF16),
                   w_out[i].astype(BF16))

        h = _ffn(h, ffn2_norm[i].reshape(1, d), ffn2_w13[i].astype(BF16), ffn2_w2[i].astype(BF16))

    out = _final_norm(h, final_norm.reshape(1, d)).reshape(padded, batch, d)
    return jnp.swapaxes(out[N_META:length], 0, 1)
```

```python
import functools
import math

import jax
import jax.numpy as jnp
from jax import lax
from jax.experimental import pallas as pl
from jax.experimental.pallas import tpu as pltpu

F32 = jnp.float32
BF16 = jnp.bfloat16

N_META = 16
SSM_GROUP = 16
SSM_STATE = 64
CONV_WIDTH = 31
N_HEADS = 8
HEAD_DIM = 64
RMS_EPS = 1e-6
LN_EPS = 1e-5

LANES = 128
VMEM_LIMIT_BYTES = 56 * 1024 * 1024

TIME_TILE = 128
TOKEN_TILE = 512
SSM_GROUP_BLOCK = 8
CONV_BLOCK = 2
ATTN_HEADS_PER_GROUP = 2
ATTN_SKIP_DECAY = 100.0


def _params(semantics):
    return pltpu.CompilerParams(dimension_semantics=semantics,
                                vmem_limit_bytes=VMEM_LIMIT_BYTES)


def _resident(shape):
    zeros = (0,) * len(shape)
    return pl.BlockSpec(shape, lambda *_: zeros, pipeline_mode=pl.Buffered(1))


def _rms(x, g):
    return x * lax.rsqrt(jnp.mean(x * x, axis=-1, keepdims=True) + RMS_EPS) * g


def _sigmoid(x):
    return 1.0 / (1.0 + jnp.exp(-x))


def _ffn_residual(h_ref, g_ref, w13_ref, w2_ref, d_ff):
    x = h_ref[...]
    xb = _rms(x, g_ref[...]).astype(BF16)
    ab = jnp.dot(xb, w13_ref[...], preferred_element_type=F32)
    a = ab[:, :d_ff]
    b = ab[:, d_ff:]
    hm = (a * _sigmoid(a) * b).astype(BF16)
    y = jnp.dot(hm, w2_ref[...], preferred_element_type=F32)
    return x + 0.5 * y


def _ffn_kernel(h_ref, g_ref, w13_ref, w2_ref, o_ref, *, d_ff):
    o_ref[...] = _ffn_residual(h_ref, g_ref, w13_ref, w2_ref, d_ff)


def _ffn_norm_kernel(h_ref, g_ref, w13_ref, w2_ref, g_out_ref, o_ref, *, d_ff):
    o_ref[...] = _rms(_ffn_residual(h_ref, g_ref, w13_ref, w2_ref, d_ff), g_out_ref[...])


def _ffn(h, g, w13, w2, g_out=None):
    rows, d = h.shape
    d_ff = w2.shape[0]
    row_spec = pl.BlockSpec((TOKEN_TILE, d), lambda i: (i, 0))
    body, extra = (_ffn_kernel, []) if g_out is None else (_ffn_norm_kernel, [g_out])
    return pl.pallas_call(
        functools.partial(body, d_ff=d_ff),
        grid=(rows // TOKEN_TILE,),
        in_specs=[row_spec, _resident((1, d)), _resident(w13.shape), _resident(w2.shape)]
                 + [_resident((1, d))] * len(extra),
        out_specs=row_spec,
        out_shape=jax.ShapeDtypeStruct((rows, d), F32),
        compiler_params=_params(("parallel",)),
        name="ffn",
    )(h, g, w13, w2, *extra)


def _block_diag(blocks):
    nb, k, r, c = blocks.shape
    eye = jnp.eye(k, dtype=blocks.dtype)
    return jnp.einsum("bkrc,kj->bkrjc", blocks, eye).reshape(nb, k * r, k * c)


def _mixer_in_kernel(h_ref, g_ref, w_ref, cw_ref, cb_ref, lng_ref, lnb_ref,
                     lre_ref, lim_ref, ldt_ref, bre_ref, bim_ref, cre_ref, cim_ref, d_ref,
                     y_ref, hc_ref, q_ref, k_ref, v_ref,
                     win_s, conv_s, are_s, aim_s, bbre_s, bbim_s, hre_s, him_s,
                     *, batch, d_ssm, d_conv, d_attn):
    steps = TOKEN_TILE // batch
    n_gb = d_ssm // (SSM_GROUP_BLOCK * SSM_GROUP)
    chan = SSM_GROUP_BLOCK * SSM_GROUP

    @pl.when(pl.program_id(0) == 0)
    def _first_tile():
        dt = jnp.exp(ldt_ref[...])
        lr = lre_ref[...]
        li = lim_ref[...]
        mag = jnp.exp(lr * dt)
        ab_re = mag * jnp.cos(li * dt)
        ab_im = mag * jnp.sin(li * dt)
        den = lr * lr + li * li
        nr = ab_re - 1.0
        ni = ab_im
        coef_re = (nr * lr + ni * li) / den
        coef_im = (ni * lr - nr * li) / den
        are_s[...] = jnp.broadcast_to(ab_re, are_s.shape)
        aim_s[...] = jnp.broadcast_to(ab_im, aim_s.shape)
        br = bre_ref[...]
        bi = bim_ref[...]
        bbre_s[...] = (coef_re * br - coef_im * bi).astype(BF16)
        bbim_s[...] = (coef_re * bi + coef_im * br).astype(BF16)
        hre_s[...] = jnp.zeros_like(hre_s)
        him_s[...] = jnp.zeros_like(him_s)
        win_s[:TOKEN_TILE, :] = jnp.zeros((TOKEN_TILE, d_conv), F32)

    xb = _rms(h_ref[...], g_ref[...]).astype(BF16)
    c0 = d_ssm
    c1 = c0 + d_conv
    c2 = c1 + d_conv
    mxu_n = 2 * LANES

    def proj(col, width=mxu_n):
        return jnp.dot(xb, w_ref[:, col:col + width], preferred_element_type=F32)

    def conv_in(p):
        cols = slice(p * mxu_n, (p + 1) * mxu_n)
        win_s[TOKEN_TILE:, cols] = proj(c0 + p * mxu_n) * _sigmoid(proj(c1 + p * mxu_n))

    u_parts = {}

    def ssm_in(p):
        u_parts[p] = proj(p * mxu_n)

    def u_of(gb):
        part = u_parts[gb * chan // mxu_n]
        off = gb * chan % mxu_n
        return part[:, off:off + chan]

    def attn_in(piece):
        col = piece * mxu_n
        val = proj(c2 + col)
        ref, off = ((q_ref, k_ref, v_ref)[col // d_attn], col % d_attn)
        if ref is q_ref:
            val = val * (1.0 / math.sqrt(HEAD_DIM))
        ref[:, off:off + mxu_n] = val.astype(BF16)

    bu = {}

    def ssm_drive(gb):
        ub = u_of(gb).astype(BF16)
        bu[gb] = (jnp.dot(ub, bbre_s[gb], preferred_element_type=F32),
                  jnp.dot(ub, bbim_s[gb], preferred_element_type=F32))

    states = {}

    def ssm_out(gb):
        ch = slice(gb * chan, (gb + 1) * chan)
        s_re, s_im = states[gb]
        y = (jnp.dot(s_re.astype(BF16), cre_ref[gb], preferred_element_type=F32)
             - jnp.dot(s_im.astype(BF16), cim_ref[gb], preferred_element_type=F32))
        y = y + d_ref[:, ch] * u_of(gb)
        y_ref[:, ch] = jax.nn.gelu(y).astype(BF16)

    def ssm_scan(gb):
        bu_re, bu_im = bu[gb]
        a_re = are_s[gb]
        a_im = aim_s[gb]
        h_re = hre_s[gb]
        h_im = him_s[gb]
        s_re, s_im = [], []
        for t in range(steps):
            rows = slice(t * batch, (t + 1) * batch)
            h_re, h_im = (a_re * h_re - a_im * h_im + bu_re[rows],
                          a_re * h_im + a_im * h_re + bu_im[rows])
            s_re.append(h_re)
            s_im.append(h_im)
        hre_s[gb] = h_re
        him_s[gb] = h_im
        states[gb] = (jnp.concatenate(s_re, axis=0), jnp.concatenate(s_im, axis=0))

    def conv_piece(cols, tb):
        rows = CONV_BLOCK * batch
        acc = None
        for j in range(CONV_WIDTH):
            r0 = TOKEN_TILE + (tb * CONV_BLOCK - (CONV_WIDTH - 1) + j) * batch
            term = cw_ref[j:j + 1, cols] * win_s[r0:r0 + rows, cols]
            acc = term if acc is None else acc + term
        conv_s[tb * rows:(tb + 1) * rows, cols] = acc

    def conv_quarter(p, part):
        cols = slice(p * mxu_n, (p + 1) * mxu_n)
        n_tb = steps // CONV_BLOCK // 4
        for tb in range(part * n_tb, (part + 1) * n_tb):
            conv_piece(cols, tb)

    mxu = ([(2, functools.partial(conv_in, p)) for p in range(d_conv // mxu_n)]
           + [(1, functools.partial(ssm_in, p)) for p in range(d_ssm // mxu_n)]
           + [(1, functools.partial(ssm_drive, gb)) for gb in range(n_gb)]
           + [(1, functools.partial(attn_in, p)) for p in range(3 * d_attn // mxu_n)])
    convs = [(None, functools.partial(conv_quarter, p, part))
             for p in range(d_conv // mxu_n) for part in range(4)]
    scans = [(gb, functools.partial(ssm_scan, gb)) for gb in range(n_gb)]
    vpu = convs[:4]
    for i in range(max(len(convs) - 4, len(scans))):
        vpu += scans[i:i + 1] + convs[4 + i:5 + i]
    mxu.pop(0)[1]()
    scanned = []
    for gb, job in vpu:
        units = 0
        while units < 2 and (scanned or mxu):
            if scanned:
                ssm_out(scanned.pop(0))
                units += 1
            else:
                cost, piece = mxu.pop(0)
                piece()
                units += cost
        job()
        if gb is not None:
            scanned.append(gb)
    for gb in scanned:
        ssm_out(gb)
    for _, piece in mxu:
        piece()
    win_s[:TOKEN_TILE, :] = win_s[TOKEN_TILE:, :]

    c = conv_s[...] + cb_ref[...]
    mu = jnp.mean(c, axis=-1, keepdims=True)
    xc = c - mu
    yc = xc * lax.rsqrt(jnp.mean(xc * xc, axis=-1, keepdims=True) + LN_EPS)
    yc = yc * lng_ref[...] + lnb_ref[...]
    hc_ref[...] = (yc * _sigmoid(yc)).astype(BF16)


def _mixer_in(h, g, w, conv_w, conv_b, ln_g, ln_b, lam_re, lam_im, log_dt, b_re, b_im, c_re, c_im,
              d_skip, batch, d_attn):
    rows, d = h.shape
    d_ssm = d_skip.shape[0]
    d_conv = conv_w.shape[1]
    groups = lam_re.shape[0]
    nb = groups // SSM_GROUP_BLOCK
    chan = SSM_GROUP_BLOCK * SSM_GROUP
    width = SSM_GROUP_BLOCK * SSM_STATE
    assert CONV_WIDTH - 1 <= TOKEN_TILE // batch and d_conv % LANES == 0

    lre = lam_re.reshape(nb, 1, width)
    lim = lam_im.reshape(nb, 1, width)
    ldt = jnp.repeat(log_dt, SSM_STATE).reshape(nb, 1, width)
    bre = _block_diag(b_re.reshape(nb, SSM_GROUP_BLOCK, SSM_STATE, SSM_GROUP).transpose(0, 1, 3, 2))
    bim = _block_diag(b_im.reshape(nb, SSM_GROUP_BLOCK, SSM_STATE, SSM_GROUP).transpose(0, 1, 3, 2))
    cre = _block_diag(c_re.reshape(nb, SSM_GROUP_BLOCK, SSM_GROUP, SSM_STATE).transpose(0, 1, 3, 2))
    cim = _block_diag(c_im.reshape(nb, SSM_GROUP_BLOCK, SSM_GROUP, SSM_STATE).transpose(0, 1, 3, 2))

    def row_spec(cols):
        return pl.BlockSpec((TOKEN_TILE, cols), lambda i: (i, 0))

    vec = lambda a: a.reshape(1, -1)
    operands = [g, w, conv_w, vec(conv_b), vec(ln_g), vec(ln_b), lre, lim, ldt, bre, bim,
                cre.astype(BF16), cim.astype(BF16), vec(d_skip)]
    return pl.pallas_call(
        functools.partial(_mixer_in_kernel, batch=batch, d_ssm=d_ssm, d_conv=d_conv, d_attn=d_attn),
        grid=(rows // TOKEN_TILE,),
        in_specs=[row_spec(d)] + [_resident(a.shape) for a in operands],
        out_specs=[row_spec(d_ssm), row_spec(d_conv)] + [row_spec(d_attn)] * 3,
        out_shape=[jax.ShapeDtypeStruct((rows, d_ssm), BF16),
                   jax.ShapeDtypeStruct((rows, d_conv), BF16)]
                  + [jax.ShapeDtypeStruct((rows, d_attn), BF16)] * 3,
        scratch_shapes=[pltpu.VMEM((2 * TOKEN_TILE, d_conv), F32), pltpu.VMEM((TOKEN_TILE, d_conv), F32),
                        pltpu.VMEM((nb, batch, width), F32), pltpu.VMEM((nb, batch, width), F32),
                        pltpu.VMEM((nb, chan, width), BF16), pltpu.VMEM((nb, chan, width), BF16),
                        pltpu.VMEM((nb, batch, width), F32), pltpu.VMEM((nb, batch, width), F32)],
        compiler_params=_params(("arbitrary",)),
        name="mixer_in",
    )(h, *operands)


def _merge_kernel(h_ref, g_ref, wg_ref, y_ref, wglu_ref, hc_ref, wpw_ref, oa_ref, wo_ref,
                  wout_ref, o_ref, *, d):
    x = h_ref[...]
    xb = _rms(x, g_ref[...]).astype(BF16)
    gates = _sigmoid(jnp.dot(xb, wg_ref[...], preferred_element_type=F32))
    sg = jnp.dot(y_ref[...], wglu_ref[...], preferred_element_type=F32)
    o_ssm = sg[:, :d] * _sigmoid(sg[:, d:])
    o_conv = jnp.dot(hc_ref[...], wpw_ref[...], preferred_element_type=F32)
    o_attn = jnp.dot(oa_ref[...], wo_ref[...], preferred_element_type=F32)
    merged = gates[:, :d] * o_ssm + gates[:, d:2 * d] * o_conv + gates[:, 2 * d:] * o_attn
    o_ref[...] = x + jnp.dot(merged.astype(BF16), wout_ref[...], preferred_element_type=F32)


def _merge(h, g, wg, y, wglu, hc, wpw, oa, wo, wout):
    rows, d = h.shape

    def row_spec(width):
        return pl.BlockSpec((TOKEN_TILE, width), lambda i: (i, 0))

    return pl.pallas_call(
        functools.partial(_merge_kernel, d=d),
        grid=(rows // TOKEN_TILE,),
        in_specs=[row_spec(d), _resident((1, d)), _resident(wg.shape),
                  row_spec(y.shape[1]), _resident(wglu.shape),
                  row_spec(hc.shape[1]), _resident(wpw.shape),
                  row_spec(oa.shape[1]), _resident(wo.shape),
                  _resident(wout.shape)],
        out_specs=row_spec(d),
        out_shape=jax.ShapeDtypeStruct((rows, d), F32),
        compiler_params=_params(("parallel",)),
        name="mixer_merge",
    )(h, g, wg, y, wglu, hc, wpw, oa, wo, wout)


def _attn_kernel(qt_ref, k_ref, vt_ref, ot_ref):
    blk = TIME_TILE
    sub = 8
    n_sub = blk // sub
    hpg = ATTN_HEADS_PER_GROUP
    gw = hpg * HEAD_DIM
    cols = hpg * blk
    qi = pl.program_id(1)

    row = lax.broadcasted_iota(jnp.int32, (blk, cols), 0)
    key = (row & (sub - 1)) * n_sub + row // sub
    qry = lax.broadcasted_iota(jnp.int32, (blk, cols), 1) & (blk - 1)
    strictly_before = key < qry
    sublane = lax.broadcasted_iota(jnp.int32, (sub, cols), 0)

    chan_head = lax.broadcasted_iota(jnp.int32, (gw, blk), 0) // HEAD_DIM
    q_groups = []
    for g in range(N_HEADS // hpg):
        qg = qt_ref[0, 0, g * gw:(g + 1) * gw, :]
        zero = jnp.zeros_like(qg)
        q_groups.append(jnp.concatenate(
            [jnp.where(chan_head == hl, qg, zero) for hl in range(hpg)], axis=1))

    def softplus_suffix(sp, run):
        x = sp.reshape(n_sub, sub, cols)
        same_sublane = [None] * n_sub
        tot = jnp.zeros((sub, cols), F32)
        for g in reversed(range(n_sub)):
            same_sublane[g] = tot
            tot = tot + x[g]
        s = tot
        for shift in (1, 2, 4):
            moved = pltpu.roll(s, sub - shift, axis=0)
            s = s + jnp.where(sublane < sub - shift, moved, 0.0)
        higher = s - tot + run
        later = jnp.stack([part + higher for part in same_sublane], axis=0)
        return later.reshape(blk, cols), run + s[0:1, :]

    def scores(j):
        k0 = pl.multiple_of(j * blk, blk)
        return [jnp.dot(k_ref[0, pl.ds(k0, blk), g * gw:(g + 1) * gw], q_groups[g],
                        preferred_element_type=F32) for g in range(N_HEADS // hpg)]

    def block(j, zts, accs, runs, masked):
        new_accs, new_runs = [], []
        for g in range(N_HEADS // hpg):
            zt = zts[g]
            sp = jnp.maximum(zt, 0.0) + jnp.log(1.0 + jnp.exp(-jnp.abs(zt)))
            if masked:
                sp = jnp.where(strictly_before, sp, 0.0)
            later, run = softplus_suffix(sp, runs[g])
            w = jnp.exp(zt - sp - later)
            if masked:
                w = jnp.where(strictly_before, w, 0.0)
            w = w.astype(BF16)
            new_runs.append(run)
            for p in range(hpg // 2):
                c0 = 2 * p * blk
                h0 = g * hpg + 2 * p
                res = jnp.dot(vt_ref[0, j, h0 * HEAD_DIM:(h0 + 2) * HEAD_DIM, :], w[:, c0:c0 + 2 * blk],
                              preferred_element_type=F32)
                new_accs.append(accs[h0] + res[:HEAD_DIM, :blk])
                new_accs.append(accs[h0 + 1] + res[HEAD_DIM:, blk:])
        return new_accs, new_runs

    def most_decay(runs):
        m = runs[0]
        for run in runs[1:]:
            m = jnp.minimum(m, run)
        return jnp.min(m)

    accs = [jnp.zeros((HEAD_DIM, blk), F32)] * N_HEADS
    runs = [jnp.zeros((1, cols), F32)] * (N_HEADS // hpg)
    zts = scores(qi)
    nxt = scores(jnp.maximum(qi - 1, 0))
    accs, runs = block(qi, zts, accs, runs, True)

    def more(carry):
        n, m = carry[:2]
        return jnp.logical_and(n < qi, m < ATTN_SKIP_DECAY)

    def earlier(carry):
        n, _, z, a, r = carry
        z_next = scores(jnp.maximum(qi - 2 - n, 0))
        a, r = block(qi - 1 - n, z, a, r, False)
        return n + 1, most_decay(r), z_next, a, r

    accs = lax.while_loop(more, earlier, (jnp.int32(0), most_decay(runs), nxt, accs, runs))[3]
    ot_ref[0, 0] = jnp.concatenate(accs, axis=0).astype(BF16)


def _attention(qt, k, vt):
    batch, n_blk, d_attn, _ = qt.shape
    length = k.shape[1]
    tile_spec = pl.BlockSpec((1, 1, d_attn, TIME_TILE), lambda b, i: (b, i, 0, 0))
    return pl.pallas_call(
        _attn_kernel,
        grid=(batch, n_blk),
        in_specs=[tile_spec,
                  pl.BlockSpec((1, length, d_attn), lambda b, i: (b, 0, 0)),
                  pl.BlockSpec((1, n_blk, d_attn, TIME_TILE), lambda b, i: (b, 0, 0, 0))],
        out_specs=tile_spec,
        out_shape=jax.ShapeDtypeStruct((batch, n_blk, d_attn, TIME_TILE), BF16),
        compiler_params=_params(("parallel", "parallel")),
        name="stick_breaking_attention",
    )(qt, k, vt)


def kernel(x, meta_tokens, ffn1_norm, ffn1_w13, ffn1_w2, mix_norm, w_in, ssm_lam_re, ssm_lam_im, ssm_log_dt, ssm_b_re, ssm_b_im, ssm_c_re, ssm_c_im, ssm_d, ssm_w_glu, conv_w, conv_b, conv_ln_g, conv_ln_b, conv_w_out, attn_w_o, w_out, ffn2_norm, ffn2_w13, ffn2_w2, final_norm):
    batch, seq, d = x.shape
    depth = w_in.shape[0]
    d_ssm = ssm_d.shape[1]
    d_conv = conv_w.shape[2]
    d_attn = attn_w_o.shape[1]
    n_mix = d_ssm + 2 * d_conv + 3 * d_attn
    assert d_attn == N_HEADS * HEAD_DIM and batch % 16 == 0 and d_ssm % (SSM_GROUP_BLOCK * SSM_GROUP) == 0
    length = seq + N_META
    padded = -(-length // TIME_TILE) * TIME_TILE
    n_blk = padded // TIME_TILE
    rows = padded * batch
    assert rows % TOKEN_TILE == 0

    meta = jnp.broadcast_to(meta_tokens[:, None, :].astype(x.dtype), (N_META, batch, d))
    h = jnp.pad(jnp.swapaxes(x, 0, 1), ((N_META, padded - length), (0, 0), (0, 0)))
    h = lax.dynamic_update_slice(h, meta, (0, 0, 0)).reshape(rows, d)

    for i in range(depth):
        h = _ffn(h, ffn1_norm[i].reshape(1, d), ffn1_w13[i].astype(BF16), ffn1_w2[i].astype(BF16))

        g_mix = mix_norm[i].reshape(1, d)
        w_in_b = w_in[i].astype(BF16)
        y, hc, q, k, v = _mixer_in(h, g_mix, w_in_b[:, :n_mix], conv_w[i], conv_b[i], conv_ln_g[i],
                                   conv_ln_b[i], ssm_lam_re[i], ssm_lam_im[i], ssm_log_dt[i],
                                   ssm_b_re[i], ssm_b_im[i], ssm_c_re[i], ssm_c_im[i], ssm_d[i],
                                   batch, d_attn)
        n_sub = TIME_TILE // 8
        blocked = (n_blk, TIME_TILE, batch, d_attn)
        keyed = (n_blk, 8, n_sub, batch, d_attn)
        qt = q.reshape(blocked).transpose(2, 0, 3, 1)
        vt = v.reshape(keyed).transpose(3, 0, 4, 2, 1).reshape(batch, n_blk, d_attn, TIME_TILE)
        kb = k.reshape(keyed).transpose(3, 0, 2, 1, 4).reshape(batch, padded, d_attn)
        oa = _attention(qt, kb, vt).transpose(1, 3, 0, 2).reshape(rows, d_attn)
        h = _merge(h, g_mix, w_in_b[:, n_mix:], y, ssm_w_glu[i].astype(BF16),
                   hc, conv_w_out[i].astype(BF16), oa, attn_w_o[i].astype(BF16),
                   w_out[i].astype(BF16))

        g_out = final_norm.reshape(1, d) if i == depth - 1 else None
        h = _ffn(h, ffn2_norm[i].reshape(1, d), ffn2_w13[i].astype(BF16), ffn2_w2[i].astype(BF16),
                 g_out)

    return jnp.swapaxes(h.reshape(padded, batch, d)[N_META:length], 0, 1)
```

```python
import functools
import math

import jax
import jax.numpy as jnp
from jax import lax
from jax.experimental import pallas as pl
from jax.experimental.pallas import tpu as pltpu

F32 = jnp.float32
BF16 = jnp.bfloat16

N_META = 16
SSM_GROUP = 16
SSM_STATE = 64
CONV_WIDTH = 31
N_HEADS = 8
HEAD_DIM = 64
RMS_EPS = 1e-6
LN_EPS = 1e-5

LANES = 128
VMEM_LIMIT_BYTES = 56 * 1024 * 1024

TIME_TILE = 128
TOKEN_TILE = 512
SSM_GROUP_BLOCK = 8
CONV_BLOCK = 2
ATTN_HEADS_PER_GROUP = 2
ATTN_QUERY_BLOCKS = 1
ATTN_SKIP_DECAY = 100.0


def _params(semantics):
    return pltpu.CompilerParams(dimension_semantics=semantics,
                                vmem_limit_bytes=VMEM_LIMIT_BYTES)


def _resident(shape):
    zeros = (0,) * len(shape)
    return pl.BlockSpec(shape, lambda *_: zeros, pipeline_mode=pl.Buffered(1))


def _resident_cols(w, start, width):
    if start % width == 0 and w.shape[1] % width == 0:
        col = start // width
        return w, pl.BlockSpec((w.shape[0], width), lambda *_: (0, col),
                               pipeline_mode=pl.Buffered(1))
    part = w[:, start:start + width]
    return part, _resident(part.shape)


def _rms(x, g):
    return x * lax.rsqrt(jnp.mean(x * x, axis=-1, keepdims=True) + RMS_EPS) * g


def _sigmoid(x):
    return 1.0 / (1.0 + jnp.exp(-x))


def _ffn_residual(h_ref, g_ref, w13_ref, w2_ref, d_ff):
    x = h_ref[...]
    xb = _rms(x, g_ref[...]).astype(BF16)
    ab = jnp.dot(xb, w13_ref[...], preferred_element_type=F32)
    a = ab[:, :d_ff]
    b = ab[:, d_ff:]
    hm = (a * _sigmoid(a) * b).astype(BF16)
    y = jnp.dot(hm, w2_ref[...], preferred_element_type=F32)
    return x + 0.5 * y


def _ffn_kernel(h_ref, g_ref, w13_ref, w2_ref, o_ref, *, d_ff):
    o_ref[...] = _ffn_residual(h_ref, g_ref, w13_ref, w2_ref, d_ff)


def _ffn_norm_kernel(h_ref, g_ref, w13_ref, w2_ref, g_out_ref, o_ref, *, d_ff):
    o_ref[...] = _rms(_ffn_residual(h_ref, g_ref, w13_ref, w2_ref, d_ff), g_out_ref[...])


def _ffn(h, g, w13, w2, g_out=None):
    rows, d = h.shape
    d_ff = w2.shape[0]
    row_spec = pl.BlockSpec((TOKEN_TILE, d), lambda i: (i, 0))
    body, extra = (_ffn_kernel, []) if g_out is None else (_ffn_norm_kernel, [g_out])
    return pl.pallas_call(
        functools.partial(body, d_ff=d_ff),
        grid=(rows // TOKEN_TILE,),
        in_specs=[row_spec, _resident((1, d)), _resident(w13.shape), _resident(w2.shape)]
                 + [_resident((1, d))] * len(extra),
        out_specs=row_spec,
        out_shape=jax.ShapeDtypeStruct((rows, d), F32),
        compiler_params=_params(("parallel",)),
        name="ffn",
    )(h, g, w13, w2, *extra)


def _block_diag(blocks):
    nb, k, r, c = blocks.shape
    eye = jnp.eye(k, dtype=blocks.dtype)
    return jnp.einsum("bkrc,kj->bkrjc", blocks, eye).reshape(nb, k * r, k * c)


def _mixer_in_kernel(h_ref, g_ref, w_ref, cw_ref, cb_ref, lng_ref, lnb_ref,
                     lre_ref, lim_ref, ldt_ref, bre_ref, bim_ref, cre_ref, cim_ref, d_ref,
                     y_ref, hc_ref, q_ref, k_ref, v_ref,
                     win_s, conv_s, are_s, aim_s, bbre_s, bbim_s, hre_s, him_s,
                     *, batch, d_ssm, d_conv, d_attn):
    steps = TOKEN_TILE // batch
    n_gb = d_ssm // (SSM_GROUP_BLOCK * SSM_GROUP)
    chan = SSM_GROUP_BLOCK * SSM_GROUP

    @pl.when(pl.program_id(0) == 0)
    def _first_tile():
        dt = jnp.exp(ldt_ref[...])
        lr = lre_ref[...]
        li = lim_ref[...]
        mag = jnp.exp(lr * dt)
        ab_re = mag * jnp.cos(li * dt)
        ab_im = mag * jnp.sin(li * dt)
        den = lr * lr + li * li
        nr = ab_re - 1.0
        ni = ab_im
        coef_re = (nr * lr + ni * li) / den
        coef_im = (ni * lr - nr * li) / den
        are_s[...] = jnp.broadcast_to(ab_re, are_s.shape)
        aim_s[...] = jnp.broadcast_to(ab_im, aim_s.shape)
        br = bre_ref[...]
        bi = bim_ref[...]
        bbre_s[...] = (coef_re * br - coef_im * bi).astype(BF16)
        bbim_s[...] = (coef_re * bi + coef_im * br).astype(BF16)
        hre_s[...] = jnp.zeros_like(hre_s)
        him_s[...] = jnp.zeros_like(him_s)
        win_s[:TOKEN_TILE, :] = jnp.zeros((TOKEN_TILE, d_conv), F32)

    xb = _rms(h_ref[...], g_ref[...]).astype(BF16)
    c0 = d_ssm
    c1 = c0 + d_conv
    c2 = c1 + d_conv
    mxu_n = 2 * LANES

    def proj(col, width=mxu_n):
        return jnp.dot(xb, w_ref[:, col:col + width], preferred_element_type=F32)

    def conv_in(p):
        cols = slice(p * mxu_n, (p + 1) * mxu_n)
        win_s[TOKEN_TILE:, cols] = proj(c0 + p * mxu_n) * _sigmoid(proj(c1 + p * mxu_n))

    u_parts = {}

    def ssm_in(p):
        u_parts[p] = proj(p * mxu_n)

    def u_of(gb):
        part = u_parts[gb * chan // mxu_n]
        off = gb * chan % mxu_n
        return part[:, off:off + chan]

    def attn_in(piece):
        col = piece * mxu_n
        val = proj(c2 + col)
        ref, off = ((q_ref, k_ref, v_ref)[col // d_attn], col % d_attn)
        if ref is q_ref:
            val = val * (1.0 / math.sqrt(HEAD_DIM))
        ref[:, off:off + mxu_n] = val.astype(BF16)

    bu = {}

    def ssm_drive(gb):
        ub = u_of(gb).astype(BF16)
        bu[gb] = (jnp.dot(ub, bbre_s[gb], preferred_element_type=F32),
                  jnp.dot(ub, bbim_s[gb], preferred_element_type=F32))

    states = {}

    def ssm_out(gb):
        ch = slice(gb * chan, (gb + 1) * chan)
        s_re, s_im = states[gb]
        y = (jnp.dot(s_re.astype(BF16), cre_ref[gb], preferred_element_type=F32)
             - jnp.dot(s_im.astype(BF16), cim_ref[gb], preferred_element_type=F32))
        y = y + d_ref[:, ch] * u_of(gb)
        y_ref[:, ch] = jax.nn.gelu(y).astype(BF16)

    def ssm_scan(gb):
        bu_re, bu_im = bu[gb]
        a_re = are_s[gb]
        a_im = aim_s[gb]
        h_re = hre_s[gb]
        h_im = him_s[gb]
        s_re, s_im = [], []
        for t in range(steps):
            rows = slice(t * batch, (t + 1) * batch)
            h_re, h_im = (a_re * h_re - a_im * h_im + bu_re[rows],
                          a_re * h_im + a_im * h_re + bu_im[rows])
            s_re.append(h_re)
            s_im.append(h_im)
        hre_s[gb] = h_re
        him_s[gb] = h_im
        states[gb] = (jnp.concatenate(s_re, axis=0), jnp.concatenate(s_im, axis=0))

    def conv_piece(cols, tb):
        rows = CONV_BLOCK * batch
        acc = None
        for j in range(CONV_WIDTH):
            r0 = TOKEN_TILE + (tb * CONV_BLOCK - (CONV_WIDTH - 1) + j) * batch
            term = cw_ref[j:j + 1, cols] * win_s[r0:r0 + rows, cols]
            acc = term if acc is None else acc + term
        conv_s[tb * rows:(tb + 1) * rows, cols] = acc

    def conv_quarter(p, part):
        cols = slice(p * mxu_n, (p + 1) * mxu_n)
        n_tb = steps // CONV_BLOCK // 4
        for tb in range(part * n_tb, (part + 1) * n_tb):
            conv_piece(cols, tb)

    mxu = ([(2, functools.partial(conv_in, p)) for p in range(d_conv // mxu_n)]
           + [(1, functools.partial(ssm_in, p)) for p in range(d_ssm // mxu_n)]
           + [(1, functools.partial(ssm_drive, gb)) for gb in range(n_gb)]
           + [(1, functools.partial(attn_in, p)) for p in range(3 * d_attn // mxu_n)])
    convs = [(None, functools.partial(conv_quarter, p, part))
             for p in range(d_conv // mxu_n) for part in range(4)]
    scans = [(gb, functools.partial(ssm_scan, gb)) for gb in range(n_gb)]
    vpu = convs[:4]
    for i in range(max(len(convs) - 4, len(scans))):
        vpu += scans[i:i + 1] + convs[4 + i:5 + i]
    mxu.pop(0)[1]()
    scanned = []
    for gb, job in vpu:
        units = 0
        while units < 2 and (scanned or mxu):
            if scanned:
                ssm_out(scanned.pop(0))
                units += 1
            else:
                cost, piece = mxu.pop(0)
                piece()
                units += cost
        job()
        if gb is not None:
            scanned.append(gb)
    for gb in scanned:
        ssm_out(gb)
    for _, piece in mxu:
        piece()
    win_s[:TOKEN_TILE, :] = win_s[TOKEN_TILE:, :]

    c = conv_s[...] + cb_ref[...]
    mu = jnp.mean(c, axis=-1, keepdims=True)
    xc = c - mu
    yc = xc * lax.rsqrt(jnp.mean(xc * xc, axis=-1, keepdims=True) + LN_EPS)
    yc = yc * lng_ref[...] + lnb_ref[...]
    hc_ref[...] = (yc * _sigmoid(yc)).astype(BF16)


def _mixer_in(h, g, w, conv_w, conv_b, ln_g, ln_b, lam_re, lam_im, log_dt, b_re, b_im, c_re, c_im,
              d_skip, batch, d_attn):
    rows, d = h.shape
    d_ssm = d_skip.shape[0]
    d_conv = conv_w.shape[1]
    groups = lam_re.shape[0]
    nb = groups // SSM_GROUP_BLOCK
    chan = SSM_GROUP_BLOCK * SSM_GROUP
    width = SSM_GROUP_BLOCK * SSM_STATE
    assert CONV_WIDTH - 1 <= TOKEN_TILE // batch and d_conv % LANES == 0

    lre = lam_re.reshape(nb, 1, width)
    lim = lam_im.reshape(nb, 1, width)
    ldt = jnp.repeat(log_dt, SSM_STATE).reshape(nb, 1, width)
    bre = _block_diag(b_re.reshape(nb, SSM_GROUP_BLOCK, SSM_STATE, SSM_GROUP).transpose(0, 1, 3, 2))
    bim = _block_diag(b_im.reshape(nb, SSM_GROUP_BLOCK, SSM_STATE, SSM_GROUP).transpose(0, 1, 3, 2))
    cre = _block_diag(c_re.reshape(nb, SSM_GROUP_BLOCK, SSM_GROUP, SSM_STATE).transpose(0, 1, 3, 2))
    cim = _block_diag(c_im.reshape(nb, SSM_GROUP_BLOCK, SSM_GROUP, SSM_STATE).transpose(0, 1, 3, 2))

    def row_spec(cols):
        return pl.BlockSpec((TOKEN_TILE, cols), lambda i: (i, 0))

    vec = lambda a: a.reshape(1, -1)
    w_mix, w_spec = _resident_cols(w, 0, d_ssm + 2 * d_conv + 3 * d_attn)
    operands = [g, w_mix, conv_w, vec(conv_b), vec(ln_g), vec(ln_b), lre, lim, ldt, bre, bim,
                cre.astype(BF16), cim.astype(BF16), vec(d_skip)]
    specs = [_resident(a.shape) for a in operands]
    specs[1] = w_spec
    return pl.pallas_call(
        functools.partial(_mixer_in_kernel, batch=batch, d_ssm=d_ssm, d_conv=d_conv, d_attn=d_attn),
        grid=(rows // TOKEN_TILE,),
        in_specs=[row_spec(d)] + specs,
        out_specs=[row_spec(d_ssm), row_spec(d_conv)] + [row_spec(d_attn)] * 3,
        out_shape=[jax.ShapeDtypeStruct((rows, d_ssm), BF16),
                   jax.ShapeDtypeStruct((rows, d_conv), BF16)]
                  + [jax.ShapeDtypeStruct((rows, d_attn), BF16)] * 3,
        scratch_shapes=[pltpu.VMEM((2 * TOKEN_TILE, d_conv), F32), pltpu.VMEM((TOKEN_TILE, d_conv), F32),
                        pltpu.VMEM((nb, batch, width), F32), pltpu.VMEM((nb, batch, width), F32),
                        pltpu.VMEM((nb, chan, width), BF16), pltpu.VMEM((nb, chan, width), BF16),
                        pltpu.VMEM((nb, batch, width), F32), pltpu.VMEM((nb, batch, width), F32)],
        compiler_params=_params(("arbitrary",)),
        name="mixer_in",
    )(h, *operands)


def _merge_kernel(h_ref, g_ref, wg_ref, y_ref, wglu_ref, hc_ref, wpw_ref, oa_ref, wo_ref,
                  wout_ref, o_ref, *, d):
    x = h_ref[...]
    xb = _rms(x, g_ref[...]).astype(BF16)
    gates = _sigmoid(jnp.dot(xb, wg_ref[...], preferred_element_type=F32))
    sg = jnp.dot(y_ref[...], wglu_ref[...], preferred_element_type=F32)
    o_ssm = sg[:, :d] * _sigmoid(sg[:, d:])
    o_conv = jnp.dot(hc_ref[...], wpw_ref[...], preferred_element_type=F32)
    o_attn = jnp.dot(oa_ref[...], wo_ref[...], preferred_element_type=F32)
    merged = gates[:, :d] * o_ssm + gates[:, d:2 * d] * o_conv + gates[:, 2 * d:] * o_attn
    o_ref[...] = x + jnp.dot(merged.astype(BF16), wout_ref[...], preferred_element_type=F32)


def _merge(h, g, w_in, gate_col, y, wglu, hc, wpw, oa, wo, wout):
    rows, d = h.shape

    def row_spec(width):
        return pl.BlockSpec((TOKEN_TILE, width), lambda i: (i, 0))

    wg, wg_spec = _resident_cols(w_in, gate_col, w_in.shape[1] - gate_col)
    return pl.pallas_call(
        functools.partial(_merge_kernel, d=d),
        grid=(rows // TOKEN_TILE,),
        in_specs=[row_spec(d), _resident((1, d)), wg_spec,
                  row_spec(y.shape[1]), _resident(wglu.shape),
                  row_spec(hc.shape[1]), _resident(wpw.shape),
                  row_spec(oa.shape[1]), _resident(wo.shape),
                  _resident(wout.shape)],
        out_specs=row_spec(d),
        out_shape=jax.ShapeDtypeStruct((rows, d), F32),
        compiler_params=_params(("parallel",)),
        name="mixer_merge",
    )(h, g, wg, y, wglu, hc, wpw, oa, wo, wout)


def _attn_kernel(qt_ref, k_ref, vt_ref, o_ref):
    blk = TIME_TILE
    qb = ATTN_QUERY_BLOCKS
    tq = qb * blk
    sub = 8
    n_sub = blk // sub
    hpg = ATTN_HEADS_PER_GROUP
    gw = hpg * HEAD_DIM
    cols = hpg * tq
    newest = pl.program_id(1) * qb + qb - 1

    row = lax.broadcasted_iota(jnp.int32, (blk, cols), 0)
    key = (row & (sub - 1)) * n_sub + row // sub
    qry = lax.broadcasted_iota(jnp.int32, (blk, cols), 1) & (tq - 1)
    causal = [key + (qb - 1 - m) * blk < qry for m in range(qb)]
    sublane = lax.broadcasted_iota(jnp.int32, (sub, cols), 0)

    chan_head = lax.broadcasted_iota(jnp.int32, (gw, tq), 0) // HEAD_DIM
    q_groups = []
    for g in range(N_HEADS // hpg):
        qg = qt_ref[0, 0, g * gw:(g + 1) * gw, :]
        zero = jnp.zeros_like(qg)
        q_groups.append(jnp.concatenate(
            [jnp.where(chan_head == hl, qg, zero) for hl in range(hpg)], axis=1))

    def softplus_suffix(sp, run):
        x = sp.reshape(n_sub, sub, cols)
        same_sublane = [None] * n_sub
        tot = jnp.zeros((sub, cols), F32)
        for g in reversed(range(n_sub)):
            same_sublane[g] = tot
            tot = tot + x[g]
        s = tot
        for shift in (1, 2, 4):
            moved = pltpu.roll(s, sub - shift, axis=0)
            s = s + jnp.where(sublane < sub - shift, moved, 0.0)
        higher = s - tot + run
        later = jnp.stack([part + higher for part in same_sublane], axis=0)
        return later.reshape(blk, cols), run + s[0:1, :]

    def scores(j):
        k0 = pl.multiple_of(j * blk, blk)
        return [jnp.dot(k_ref[0, pl.ds(k0, blk), g * gw:(g + 1) * gw], q_groups[g],
                        preferred_element_type=F32) for g in range(N_HEADS // hpg)]

    def block(j, zts, accs, runs, visible=None):
        new_accs, new_runs = [], []
        for g in range(N_HEADS // hpg):
            zt = zts[g]
            sp = jnp.maximum(zt, 0.0) + jnp.log(1.0 + jnp.exp(-jnp.abs(zt)))
            if visible is not None:
                sp = jnp.where(visible, sp, 0.0)
            later, run = softplus_suffix(sp, runs[g])
            w = jnp.exp(zt - sp - later)
            if visible is not None:
                w = jnp.where(visible, w, 0.0)
            w = w.astype(BF16)
            new_runs.append(run)
            for p in range(hpg // 2):
                c0 = 2 * p * tq
                h0 = g * hpg + 2 * p
                res = jnp.dot(vt_ref[0, j, h0 * HEAD_DIM:(h0 + 2) * HEAD_DIM, :], w[:, c0:c0 + 2 * tq],
                              preferred_element_type=F32)
                new_accs.append(accs[h0] + res[:HEAD_DIM, :tq])
                new_accs.append(accs[h0 + 1] + res[HEAD_DIM:, tq:])
        return new_accs, new_runs

    def most_decay(runs):
        m = runs[0]
        for run in runs[1:]:
            m = jnp.minimum(m, run)
        return jnp.min(m)

    accs = [jnp.zeros((HEAD_DIM, tq), F32)] * N_HEADS
    runs = [jnp.zeros((1, cols), F32)] * (N_HEADS // hpg)
    zts = scores(newest)
    for m in range(qb):
        nxt = scores(jnp.maximum(newest - m - 1, 0))
        accs, runs = block(newest - m, zts, accs, runs, causal[m])
        zts = nxt

    def more(carry):
        j, m = carry[:2]
        return jnp.logical_and(j >= 0, m < ATTN_SKIP_DECAY)

    def earlier(carry):
        j, _, z, a, r = carry
        z_next = scores(jnp.maximum(j - 1, 0))
        a, r = block(j, z, a, r)
        return j - 1, most_decay(r), z_next, a, r

    accs = lax.while_loop(more, earlier, (newest - qb, most_decay(runs), zts, accs, runs))[3]
    o_ref[...] = jnp.concatenate(accs, axis=0).T.astype(BF16)


def _attention(qt, k, vt):
    batch, n_tiles, d_attn, tq = qt.shape
    return pl.pallas_call(
        _attn_kernel,
        grid=(batch, n_tiles),
        in_specs=[pl.BlockSpec((1, 1, d_attn, tq), lambda b, i: (b, i, 0, 0)),
                  pl.BlockSpec((1,) + k.shape[1:], lambda b, i: (b, 0, 0)),
                  pl.BlockSpec((1,) + vt.shape[1:], lambda b, i: (b, 0, 0, 0))],
        out_specs=pl.BlockSpec((tq, d_attn), lambda b, i: (i, b)),
        out_shape=jax.ShapeDtypeStruct((n_tiles * tq, batch * d_attn), BF16),
        compiler_params=_params(("parallel", "parallel")),
        name="stick_breaking_attention",
    )(qt, k, vt)


def kernel(x, meta_tokens, ffn1_norm, ffn1_w13, ffn1_w2, mix_norm, w_in, ssm_lam_re, ssm_lam_im, ssm_log_dt, ssm_b_re, ssm_b_im, ssm_c_re, ssm_c_im, ssm_d, ssm_w_glu, conv_w, conv_b, conv_ln_g, conv_ln_b, conv_w_out, attn_w_o, w_out, ffn2_norm, ffn2_w13, ffn2_w2, final_norm):
    batch, seq, d = x.shape
    depth = w_in.shape[0]
    d_ssm = ssm_d.shape[1]
    d_conv = conv_w.shape[2]
    d_attn = attn_w_o.shape[1]
    n_mix = d_ssm + 2 * d_conv + 3 * d_attn
    assert d_attn == N_HEADS * HEAD_DIM and batch % 16 == 0 and d_ssm % (SSM_GROUP_BLOCK * SSM_GROUP) == 0
    length = seq + N_META
    padded = -(-length // TIME_TILE) * TIME_TILE
    n_blk = padded // TIME_TILE
    rows = padded * batch
    assert rows % TOKEN_TILE == 0

    meta = jnp.broadcast_to(meta_tokens[:, None, :].astype(x.dtype), (N_META, batch, d))
    h = jnp.pad(jnp.swapaxes(x, 0, 1), ((N_META, padded - length), (0, 0), (0, 0)))
    h = lax.dynamic_update_slice(h, meta, (0, 0, 0)).reshape(rows, d)

    for i in range(depth):
        h = _ffn(h, ffn1_norm[i].reshape(1, d), ffn1_w13[i].astype(BF16), ffn1_w2[i].astype(BF16))

        g_mix = mix_norm[i].reshape(1, d)
        w_in_b = w_in[i].astype(BF16)
        y, hc, q, k, v = _mixer_in(h, g_mix, w_in_b, conv_w[i], conv_b[i], conv_ln_g[i],
                                   conv_ln_b[i], ssm_lam_re[i], ssm_lam_im[i], ssm_log_dt[i],
                                   ssm_b_re[i], ssm_b_im[i], ssm_c_re[i], ssm_c_im[i], ssm_d[i],
                                   batch, d_attn)
        n_sub = TIME_TILE // 8
        tq = ATTN_QUERY_BLOCKS * TIME_TILE
        n_tiles = -(-padded // tq)
        n_kb = n_tiles * ATTN_QUERY_BLOCKS
        grow = lambda a: jnp.pad(a.reshape(padded, batch, d_attn),
                                 ((0, n_tiles * tq - padded), (0, 0), (0, 0)))
        keyed = (n_kb, 8, n_sub, batch, d_attn)
        qt = grow(q).reshape(n_tiles, tq, batch, d_attn).transpose(2, 0, 3, 1)
        vt = grow(v).reshape(keyed).transpose(3, 0, 4, 2, 1).reshape(batch, n_kb, d_attn, TIME_TILE)
        kb = grow(k).reshape(keyed).transpose(3, 0, 2, 1, 4).reshape(batch, n_kb * TIME_TILE, d_attn)
        oa = _attention(qt, kb, vt)[:padded].reshape(rows, d_attn)
        h = _merge(h, g_mix, w_in_b, n_mix, y, ssm_w_glu[i].astype(BF16),
                   hc, conv_w_out[i].astype(BF16), oa, attn_w_o[i].astype(BF16),
                   w_out[i].astype(BF16))

        g_out = final_norm.reshape(1, d) if i == depth - 1 else None
        h = _ffn(h, ffn2_norm[i].reshape(1, d), ffn2_w13[i].astype(BF16), ffn2_w2[i].astype(BF16),
                 g_out)

    return jnp.swapaxes(h.reshape(padded, batch, d)[N_META:length], 0, 1)
```

```python
import functools
import math

import jax
import jax.numpy as jnp
from jax import lax
from jax.experimental import pallas as pl
from jax.experimental.pallas import tpu as pltpu

F32 = jnp.float32
BF16 = jnp.bfloat16

N_META = 16
SSM_GROUP = 16
SSM_STATE = 64
CONV_WIDTH = 31
N_HEADS = 8
HEAD_DIM = 64
RMS_EPS = 1e-6
LN_EPS = 1e-5

LANES = 128
VMEM_LIMIT_BYTES = 56 * 1024 * 1024

TIME_TILE = 128
TOKEN_TILE = 512
SSM_GROUP_BLOCK = 8
CONV_BLOCK = 2
ATTN_HEADS_PER_GROUP = 2
ATTN_QUERY_BLOCKS = 1
ATTN_SKIP_DECAY = 100.0


def _params(semantics):
    return pltpu.CompilerParams(dimension_semantics=semantics,
                                vmem_limit_bytes=VMEM_LIMIT_BYTES)


def _resident(shape):
    zeros = (0,) * len(shape)
    return pl.BlockSpec(shape, lambda *_: zeros, pipeline_mode=pl.Buffered(1))


def _rms(x, g):
    return x * lax.rsqrt(jnp.mean(x * x, axis=-1, keepdims=True) + RMS_EPS) * g


def _sigmoid(x):
    return 1.0 / (1.0 + jnp.exp(-x))


def _ffn_residual(h_ref, g_ref, w13_ref, w2_ref, d_ff):
    x = h_ref[...]
    xb = _rms(x, g_ref[...]).astype(BF16)
    ab = jnp.dot(xb, w13_ref[...], preferred_element_type=F32)
    a = ab[:, :d_ff]
    b = ab[:, d_ff:]
    hm = (a * _sigmoid(a) * b).astype(BF16)
    y = jnp.dot(hm, w2_ref[...], preferred_element_type=F32)
    return x + 0.5 * y


def _ffn_kernel(h_ref, g_ref, w13_ref, w2_ref, o_ref, *, d_ff):
    o_ref[...] = _ffn_residual(h_ref, g_ref, w13_ref, w2_ref, d_ff)


def _ffn_norm_kernel(h_ref, g_ref, w13_ref, w2_ref, g_out_ref, o_ref, *, d_ff):
    o_ref[...] = _rms(_ffn_residual(h_ref, g_ref, w13_ref, w2_ref, d_ff), g_out_ref[...])


def _ffn(h, g, w13, w2, g_out=None):
    rows, d = h.shape
    d_ff = w2.shape[0]
    row_spec = pl.BlockSpec((TOKEN_TILE, d), lambda i: (i, 0))
    body, extra = (_ffn_kernel, []) if g_out is None else (_ffn_norm_kernel, [g_out])
    return pl.pallas_call(
        functools.partial(body, d_ff=d_ff),
        grid=(rows // TOKEN_TILE,),
        in_specs=[row_spec, _resident((1, d)), _resident(w13.shape), _resident(w2.shape)]
                 + [_resident((1, d))] * len(extra),
        out_specs=row_spec,
        out_shape=jax.ShapeDtypeStruct((rows, d), F32),
        compiler_params=_params(("parallel",)),
        name="ffn",
    )(h, g, w13, w2, *extra)


def _block_diag(blocks):
    nb, k, r, c = blocks.shape
    eye = jnp.eye(k, dtype=blocks.dtype)
    return jnp.einsum("bkrc,kj->bkrjc", blocks, eye).reshape(nb, k * r, k * c)


def _mixer_in_kernel(h_ref, g_ref, w_ref, cw_ref, cb_ref, lng_ref, lnb_ref,
                     lre_ref, lim_ref, ldt_ref, bre_ref, bim_ref, cre_ref, cim_ref, d_ref,
                     y_ref, hc_ref, q_ref, k_ref, v_ref,
                     win_s, conv_s, are_s, aim_s, bbre_s, bbim_s, hre_s, him_s,
                     *, batch, d_ssm, d_conv, d_attn):
    steps = TOKEN_TILE // batch
    n_gb = d_ssm // (SSM_GROUP_BLOCK * SSM_GROUP)
    chan = SSM_GROUP_BLOCK * SSM_GROUP

    @pl.when(pl.program_id(0) == 0)
    def _first_tile():
        dt = jnp.exp(ldt_ref[...])
        lr = lre_ref[...]
        li = lim_ref[...]
        mag = jnp.exp(lr * dt)
        ab_re = mag * jnp.cos(li * dt)
        ab_im = mag * jnp.sin(li * dt)
        den = lr * lr + li * li
        nr = ab_re - 1.0
        ni = ab_im
        coef_re = (nr * lr + ni * li) / den
        coef_im = (ni * lr - nr * li) / den
        are_s[...] = jnp.broadcast_to(ab_re, are_s.shape)
        aim_s[...] = jnp.broadcast_to(ab_im, aim_s.shape)
        br = bre_ref[...]
        bi = bim_ref[...]
        bbre_s[...] = (coef_re * br - coef_im * bi).astype(BF16)
        bbim_s[...] = (coef_re * bi + coef_im * br).astype(BF16)
        hre_s[...] = jnp.zeros_like(hre_s)
        him_s[...] = jnp.zeros_like(him_s)
        win_s[:TOKEN_TILE, :] = jnp.zeros((TOKEN_TILE, d_conv), F32)

    xb = _rms(h_ref[...], g_ref[...]).astype(BF16)
    c0 = d_ssm
    c1 = c0 + d_conv
    c2 = c1 + d_conv
    mxu_n = 2 * LANES

    def proj(col, width=mxu_n):
        return jnp.dot(xb, w_ref[:, col:col + width], preferred_element_type=F32)

    def conv_in(p):
        cols = slice(p * mxu_n, (p + 1) * mxu_n)
        win_s[TOKEN_TILE:, cols] = proj(c0 + p * mxu_n) * _sigmoid(proj(c1 + p * mxu_n))

    u_parts = {}

    def ssm_in(p):
        u_parts[p] = proj(p * mxu_n)

    def u_of(gb):
        part = u_parts[gb * chan // mxu_n]
        off = gb * chan % mxu_n
        return part[:, off:off + chan]

    def attn_in(piece):
        col = piece * mxu_n
        val = proj(c2 + col)
        ref, off = ((q_ref, k_ref, v_ref)[col // d_attn], col % d_attn)
        if ref is q_ref:
            val = val * (1.0 / math.sqrt(HEAD_DIM))
        ref[:, off:off + mxu_n] = val.astype(BF16)

    bu = {}

    def ssm_drive(gb):
        ub = u_of(gb).astype(BF16)
        bu[gb] = (jnp.dot(ub, bbre_s[gb], preferred_element_type=F32),
                  jnp.dot(ub, bbim_s[gb], preferred_element_type=F32))

    states = {}

    def ssm_out(gb):
        ch = slice(gb * chan, (gb + 1) * chan)
        s_re, s_im = states[gb]
        y = (jnp.dot(s_re.astype(BF16), cre_ref[gb], preferred_element_type=F32)
             - jnp.dot(s_im.astype(BF16), cim_ref[gb], preferred_element_type=F32))
        y = y + d_ref[:, ch] * u_of(gb)
        y_ref[:, ch] = jax.nn.gelu(y).astype(BF16)

    def ssm_scan(gb):
        bu_re, bu_im = bu[gb]
        a_re = are_s[gb]
        a_im = aim_s[gb]
        h_re = hre_s[gb]
        h_im = him_s[gb]
        s_re, s_im = [], []
        for t in range(steps):
            rows = slice(t * batch, (t + 1) * batch)
            h_re, h_im = (a_re * h_re - a_im * h_im + bu_re[rows],
                          a_re * h_im + a_im * h_re + bu_im[rows])
            s_re.append(h_re)
            s_im.append(h_im)
        hre_s[gb] = h_re
        him_s[gb] = h_im
        states[gb] = (jnp.concatenate(s_re, axis=0), jnp.concatenate(s_im, axis=0))

    def conv_piece(cols, tb):
        rows = CONV_BLOCK * batch
        acc = None
        for j in range(CONV_WIDTH):
            r0 = TOKEN_TILE + (tb * CONV_BLOCK - (CONV_WIDTH - 1) + j) * batch
            term = cw_ref[j:j + 1, cols] * win_s[r0:r0 + rows, cols]
            acc = term if acc is None else acc + term
        conv_s[tb * rows:(tb + 1) * rows, cols] = acc

    def conv_quarter(p, part):
        cols = slice(p * mxu_n, (p + 1) * mxu_n)
        n_tb = steps // CONV_BLOCK // 4
        for tb in range(part * n_tb, (part + 1) * n_tb):
            conv_piece(cols, tb)

    mxu = ([(2, functools.partial(conv_in, p)) for p in range(d_conv // mxu_n)]
           + [(1, functools.partial(ssm_in, p)) for p in range(d_ssm // mxu_n)]
           + [(1, functools.partial(ssm_drive, gb)) for gb in range(n_gb)]
           + [(1, functools.partial(attn_in, p)) for p in range(3 * d_attn // mxu_n)])
    convs = [(None, functools.partial(conv_quarter, p, part))
             for p in range(d_conv // mxu_n) for part in range(4)]
    scans = [(gb, functools.partial(ssm_scan, gb)) for gb in range(n_gb)]
    vpu = convs[:4]
    for i in range(max(len(convs) - 4, len(scans))):
        vpu += scans[i:i + 1] + convs[4 + i:5 + i]
    mxu.pop(0)[1]()
    scanned = []
    for gb, job in vpu:
        units = 0
        while units < 2 and (scanned or mxu):
            if scanned:
                ssm_out(scanned.pop(0))
                units += 1
            else:
                cost, piece = mxu.pop(0)
                piece()
                units += cost
        job()
        if gb is not None:
            scanned.append(gb)
    for gb in scanned:
        ssm_out(gb)
    for _, piece in mxu:
        piece()
    win_s[:TOKEN_TILE, :] = win_s[TOKEN_TILE:, :]

    c = conv_s[...] + cb_ref[...]
    mu = jnp.mean(c, axis=-1, keepdims=True)
    xc = c - mu
    yc = xc * lax.rsqrt(jnp.mean(xc * xc, axis=-1, keepdims=True) + LN_EPS)
    yc = yc * lng_ref[...] + lnb_ref[...]
    hc_ref[...] = (yc * _sigmoid(yc)).astype(BF16)


def _mixer_in(h, g, w, conv_w, conv_b, ln_g, ln_b, lam_re, lam_im, log_dt, b_re, b_im, c_re, c_im,
              d_skip, batch, d_attn):
    rows, d = h.shape
    d_ssm = d_skip.shape[0]
    d_conv = conv_w.shape[1]
    groups = lam_re.shape[0]
    nb = groups // SSM_GROUP_BLOCK
    chan = SSM_GROUP_BLOCK * SSM_GROUP
    width = SSM_GROUP_BLOCK * SSM_STATE
    assert CONV_WIDTH - 1 <= TOKEN_TILE // batch and d_conv % LANES == 0

    lre = lam_re.reshape(nb, 1, width)
    lim = lam_im.reshape(nb, 1, width)
    ldt = jnp.repeat(log_dt, SSM_STATE).reshape(nb, 1, width)
    bre = _block_diag(b_re.reshape(nb, SSM_GROUP_BLOCK, SSM_STATE, SSM_GROUP).transpose(0, 1, 3, 2))
    bim = _block_diag(b_im.reshape(nb, SSM_GROUP_BLOCK, SSM_STATE, SSM_GROUP).transpose(0, 1, 3, 2))
    cre = _block_diag(c_re.reshape(nb, SSM_GROUP_BLOCK, SSM_GROUP, SSM_STATE).transpose(0, 1, 3, 2))
    cim = _block_diag(c_im.reshape(nb, SSM_GROUP_BLOCK, SSM_GROUP, SSM_STATE).transpose(0, 1, 3, 2))

    def row_spec(cols):
        return pl.BlockSpec((TOKEN_TILE, cols), lambda i: (i, 0))

    vec = lambda a: a.reshape(1, -1)
    operands = [g, w, conv_w, vec(conv_b), vec(ln_g), vec(ln_b), lre, lim, ldt, bre, bim,
                cre.astype(BF16), cim.astype(BF16), vec(d_skip)]
    return pl.pallas_call(
        functools.partial(_mixer_in_kernel, batch=batch, d_ssm=d_ssm, d_conv=d_conv, d_attn=d_attn),
        grid=(rows // TOKEN_TILE,),
        in_specs=[row_spec(d)] + [_resident(a.shape) for a in operands],
        out_specs=[row_spec(d_ssm), row_spec(d_conv)] + [row_spec(d_attn)] * 3,
        out_shape=[jax.ShapeDtypeStruct((rows, d_ssm), BF16),
                   jax.ShapeDtypeStruct((rows, d_conv), BF16)]
                  + [jax.ShapeDtypeStruct((rows, d_attn), BF16)] * 3,
        scratch_shapes=[pltpu.VMEM((2 * TOKEN_TILE, d_conv), F32), pltpu.VMEM((TOKEN_TILE, d_conv), F32),
                        pltpu.VMEM((nb, batch, width), F32), pltpu.VMEM((nb, batch, width), F32),
                        pltpu.VMEM((nb, chan, width), BF16), pltpu.VMEM((nb, chan, width), BF16),
                        pltpu.VMEM((nb, batch, width), F32), pltpu.VMEM((nb, batch, width), F32)],
        compiler_params=_params(("arbitrary",)),
        name="mixer_in",
    )(h, *operands)


def _merge_kernel(h_ref, g_ref, wg_ref, y_ref, wglu_ref, hc_ref, wpw_ref, oa_ref, wo_ref,
                  wout_ref, o_ref, *, d):
    x = h_ref[...]
    xb = _rms(x, g_ref[...]).astype(BF16)
    gates = _sigmoid(jnp.dot(xb, wg_ref[...], preferred_element_type=F32))
    sg = jnp.dot(y_ref[...], wglu_ref[...], preferred_element_type=F32)
    o_ssm = sg[:, :d] * _sigmoid(sg[:, d:])
    o_conv = jnp.dot(hc_ref[...], wpw_ref[...], preferred_element_type=F32)
    o_attn = jnp.dot(oa_ref[...], wo_ref[...], preferred_element_type=F32)
    merged = gates[:, :d] * o_ssm + gates[:, d:2 * d] * o_conv + gates[:, 2 * d:] * o_attn
    o_ref[...] = x + jnp.dot(merged.astype(BF16), wout_ref[...], preferred_element_type=F32)


def _merge(h, g, wg, y, wglu, hc, wpw, oa, wo, wout):
    rows, d = h.shape

    def row_spec(width):
        return pl.BlockSpec((TOKEN_TILE, width), lambda i: (i, 0))

    return pl.pallas_call(
        functools.partial(_merge_kernel, d=d),
        grid=(rows // TOKEN_TILE,),
        in_specs=[row_spec(d), _resident((1, d)), _resident(wg.shape),
                  row_spec(y.shape[1]), _resident(wglu.shape),
                  row_spec(hc.shape[1]), _resident(wpw.shape),
                  row_spec(oa.shape[1]), _resident(wo.shape),
                  _resident(wout.shape)],
        out_specs=row_spec(d),
        out_shape=jax.ShapeDtypeStruct((rows, d), F32),
        compiler_params=_params(("parallel",)),
        name="mixer_merge",
    )(h, g, wg, y, wglu, hc, wpw, oa, wo, wout)


def _attn_kernel(qt_ref, k_ref, vt_ref, o_ref):
    blk = TIME_TILE
    qb = ATTN_QUERY_BLOCKS
    tq = qb * blk
    sub = 8
    n_sub = blk // sub
    hpg = ATTN_HEADS_PER_GROUP
    gw = hpg * HEAD_DIM
    cols = hpg * tq
    newest = pl.program_id(1) * qb + qb - 1

    row = lax.broadcasted_iota(jnp.int32, (blk, cols), 0)
    key = (row & (sub - 1)) * n_sub + row // sub
    qry = lax.broadcasted_iota(jnp.int32, (blk, cols), 1) & (tq - 1)
    causal = [key + (qb - 1 - m) * blk < qry for m in range(qb)]
    sublane = lax.broadcasted_iota(jnp.int32, (sub, cols), 0)

    chan_head = lax.broadcasted_iota(jnp.int32, (gw, tq), 0) // HEAD_DIM
    q_groups = []
    for g in range(N_HEADS // hpg):
        qg = qt_ref[0, 0, g * gw:(g + 1) * gw, :]
        zero = jnp.zeros_like(qg)
        q_groups.append(jnp.concatenate(
            [jnp.where(chan_head == hl, qg, zero) for hl in range(hpg)], axis=1))

    def softplus_suffix(sp, run):
        x = sp.reshape(n_sub, sub, cols)
        same_sublane = [None] * n_sub
        tot = jnp.zeros((sub, cols), F32)
        for g in reversed(range(n_sub)):
            same_sublane[g] = tot
            tot = tot + x[g]
        s = tot
        for shift in (1, 2, 4):
            moved = pltpu.roll(s, sub - shift, axis=0)
            s = s + jnp.where(sublane < sub - shift, moved, 0.0)
        higher = s - tot + run
        later = jnp.stack([part + higher for part in same_sublane], axis=0)
        return later.reshape(blk, cols), run + s[0:1, :]

    def scores(j):
        k0 = pl.multiple_of(j * blk, blk)
        return [jnp.dot(k_ref[0, pl.ds(k0, blk), g * gw:(g + 1) * gw], q_groups[g],
                        preferred_element_type=F32) for g in range(N_HEADS // hpg)]

    def block(j, zts, accs, runs, visible=None):
        new_accs, new_runs = [], []
        for g in range(N_HEADS // hpg):
            zt = zts[g]
            sp = jnp.maximum(zt, 0.0) + jnp.log(1.0 + jnp.exp(-jnp.abs(zt)))
            if visible is not None:
                sp = jnp.where(visible, sp, 0.0)
            later, run = softplus_suffix(sp, runs[g])
            w = jnp.exp(zt - sp - later)
            if visible is not None:
                w = jnp.where(visible, w, 0.0)
            w = w.astype(BF16)
            new_runs.append(run)
            for p in range(hpg // 2):
                c0 = 2 * p * tq
                h0 = g * hpg + 2 * p
                res = jnp.dot(vt_ref[0, j, h0 * HEAD_DIM:(h0 + 2) * HEAD_DIM, :], w[:, c0:c0 + 2 * tq],
                              preferred_element_type=F32)
                new_accs.append(accs[h0] + res[:HEAD_DIM, :tq])
                new_accs.append(accs[h0 + 1] + res[HEAD_DIM:, tq:])
        return new_accs, new_runs

    def most_decay(runs):
        m = runs[0]
        for run in runs[1:]:
            m = jnp.minimum(m, run)
        return jnp.min(m)

    accs = [jnp.zeros((HEAD_DIM, tq), F32)] * N_HEADS
    runs = [jnp.zeros((1, cols), F32)] * (N_HEADS // hpg)
    zts = scores(newest)
    for m in range(qb):
        nxt = scores(jnp.maximum(newest - m - 1, 0))
        accs, runs = block(newest - m, zts, accs, runs, causal[m])
        zts = nxt

    def more(carry):
        j, m = carry[:2]
        return jnp.logical_and(j >= 0, m < ATTN_SKIP_DECAY)

    def earlier_pair(carry):
        j, _, z, a, r = carry
        z1 = scores(jnp.maximum(j - 1, 0))
        a, r = block(j, z, a, r)
        z2 = scores(jnp.maximum(j - 2, 0))
        a1, r1 = block(jnp.maximum(j - 1, 0), z1, a, r)
        both = j >= 1
        a = [jnp.where(both, new, old) for new, old in zip(a1, a)]
        r = [jnp.where(both, new, old) for new, old in zip(r1, r)]
        return j - 2, most_decay(r), z2, a, r

    accs = lax.while_loop(more, earlier_pair, (newest - qb, most_decay(runs), zts, accs, runs))[3]
    o_ref[0, 0] = jnp.concatenate(accs, axis=0).astype(BF16)


def _attention(qt, k, vt):
    batch, n_tiles, d_attn, tq = qt.shape
    tile_spec = pl.BlockSpec((1, 1, d_attn, tq), lambda b, i: (b, i, 0, 0))
    return pl.pallas_call(
        _attn_kernel,
        grid=(batch, n_tiles),
        in_specs=[tile_spec,
                  pl.BlockSpec((1,) + k.shape[1:], lambda b, i: (b, 0, 0)),
                  pl.BlockSpec((1,) + vt.shape[1:], lambda b, i: (b, 0, 0, 0))],
        out_specs=tile_spec,
        out_shape=jax.ShapeDtypeStruct(qt.shape, BF16),
        compiler_params=_params(("parallel", "parallel")),
        name="stick_breaking_attention",
    )(qt, k, vt)


def kernel(x, meta_tokens, ffn1_norm, ffn1_w13, ffn1_w2, mix_norm, w_in, ssm_lam_re, ssm_lam_im, ssm_log_dt, ssm_b_re, ssm_b_im, ssm_c_re, ssm_c_im, ssm_d, ssm_w_glu, conv_w, conv_b, conv_ln_g, conv_ln_b, conv_w_out, attn_w_o, w_out, ffn2_norm, ffn2_w13, ffn2_w2, final_norm):
    batch, seq, d = x.shape
    depth = w_in.shape[0]
    d_ssm = ssm_d.shape[1]
    d_conv = conv_w.shape[2]
    d_attn = attn_w_o.shape[1]
    n_mix = d_ssm + 2 * d_conv + 3 * d_attn
    assert d_attn == N_HEADS * HEAD_DIM and batch % 16 == 0 and d_ssm % (SSM_GROUP_BLOCK * SSM_GROUP) == 0
    length = seq + N_META
    padded = -(-length // TIME_TILE) * TIME_TILE
    n_blk = padded // TIME_TILE
    rows = padded * batch
    assert rows % TOKEN_TILE == 0

    meta = jnp.broadcast_to(meta_tokens[:, None, :].astype(x.dtype), (N_META, batch, d))
    h = jnp.pad(jnp.swapaxes(x, 0, 1), ((N_META, padded - length), (0, 0), (0, 0)))
    h = lax.dynamic_update_slice(h, meta, (0, 0, 0)).reshape(rows, d)

    for i in range(depth):
        h = _ffn(h, ffn1_norm[i].reshape(1, d), ffn1_w13[i].astype(BF16), ffn1_w2[i].astype(BF16))

        g_mix = mix_norm[i].reshape(1, d)
        w_in_b = w_in[i].astype(BF16)
        y, hc, q, k, v = _mixer_in(h, g_mix, w_in_b[:, :n_mix], conv_w[i], conv_b[i], conv_ln_g[i],
                                   conv_ln_b[i], ssm_lam_re[i], ssm_lam_im[i], ssm_log_dt[i],
                                   ssm_b_re[i], ssm_b_im[i], ssm_c_re[i], ssm_c_im[i], ssm_d[i],
                                   batch, d_attn)
        n_sub = TIME_TILE // 8
        tq = ATTN_QUERY_BLOCKS * TIME_TILE
        n_tiles = -(-padded // tq)
        n_kb = n_tiles * ATTN_QUERY_BLOCKS
        grow = lambda a: jnp.pad(a.reshape(padded, batch, d_attn),
                                 ((0, n_tiles * tq - padded), (0, 0), (0, 0)))
        keyed = (n_kb, 8, n_sub, batch, d_attn)
        qt = grow(q).reshape(n_tiles, tq, batch, d_attn).transpose(2, 0, 3, 1)
        vt = grow(v).reshape(keyed).transpose(3, 0, 4, 2, 1).reshape(batch, n_kb, d_attn, TIME_TILE)
        kb = grow(k).reshape(keyed).transpose(3, 0, 2, 1, 4).reshape(batch, n_kb * TIME_TILE, d_attn)
        oa = _attention(qt, kb, vt).transpose(1, 3, 0, 2).reshape(n_tiles * tq, batch, d_attn)
        oa = oa[:padded].reshape(rows, d_attn)
        h = _merge(h, g_mix, w_in_b[:, n_mix:], y, ssm_w_glu[i].astype(BF16),
                   hc, conv_w_out[i].astype(BF16), oa, attn_w_o[i].astype(BF16),
                   w_out[i].astype(BF16))

        g_out = final_norm.reshape(1, d) if i == depth - 1 else None
        h = _ffn(h, ffn2_norm[i].reshape(1, d), ffn2_w13[i].astype(BF16), ffn2_w2[i].astype(BF16),
                 g_out)

    return jnp.swapaxes(h.reshape(padded, batch, d)[N_META:length], 0, 1)
```

```python
import functools
import math

import jax
import jax.numpy as jnp
from jax import lax
from jax.experimental import pallas as pl
from jax.experimental.pallas import tpu as pltpu

F32 = jnp.float32
BF16 = jnp.bfloat16

N_META = 16
SSM_GROUP = 16
SSM_STATE = 64
CONV_WIDTH = 31
N_HEADS = 8
HEAD_DIM = 64
RMS_EPS = 1e-6
LN_EPS = 1e-5

LANES = 128
VMEM_LIMIT_BYTES = 56 * 1024 * 1024

TIME_TILE = 128
TOKEN_TILE = 512
SSM_GROUP_BLOCK = 8
CONV_BLOCK = 2
ATTN_HEADS_PER_GROUP = 2
ATTN_QUERY_BLOCKS = 1
ATTN_SKIP_DECAY = 100.0


def _params(semantics):
    return pltpu.CompilerParams(dimension_semantics=semantics,
                                vmem_limit_bytes=VMEM_LIMIT_BYTES)


def _resident(shape):
    zeros = (0,) * len(shape)
    return pl.BlockSpec(shape, lambda *_: zeros, pipeline_mode=pl.Buffered(1))


def _rms(x, g):
    return x * lax.rsqrt(jnp.mean(x * x, axis=-1, keepdims=True) + RMS_EPS) * g


def _sigmoid(x):
    return 1.0 / (1.0 + jnp.exp(-x))


def _ffn_residual(h_ref, g_ref, w13_ref, w2_ref, d_ff):
    x = h_ref[...]
    xb = _rms(x, g_ref[...]).astype(BF16)
    ab = jnp.dot(xb, w13_ref[...], preferred_element_type=F32)
    a = ab[:, :d_ff]
    b = ab[:, d_ff:]
    hm = (a * _sigmoid(a) * b).astype(BF16)
    y = jnp.dot(hm, w2_ref[...], preferred_element_type=F32)
    return x + 0.5 * y


def _ffn_kernel(h_ref, g_ref, w13_ref, w2_ref, o_ref, *, d_ff):
    o_ref[...] = _ffn_residual(h_ref, g_ref, w13_ref, w2_ref, d_ff)


def _ffn_norm_kernel(h_ref, g_ref, w13_ref, w2_ref, g_out_ref, o_ref, *, d_ff):
    o_ref[...] = _rms(_ffn_residual(h_ref, g_ref, w13_ref, w2_ref, d_ff), g_out_ref[...])


def _ffn(h, g, w13, w2, g_out=None):
    rows, d = h.shape
    d_ff = w2.shape[0]
    row_spec = pl.BlockSpec((TOKEN_TILE, d), lambda i: (i, 0))
    body, extra = (_ffn_kernel, []) if g_out is None else (_ffn_norm_kernel, [g_out])
    return pl.pallas_call(
        functools.partial(body, d_ff=d_ff),
        grid=(rows // TOKEN_TILE,),
        in_specs=[row_spec, _resident((1, d)), _resident(w13.shape), _resident(w2.shape)]
                 + [_resident((1, d))] * len(extra),
        out_specs=row_spec,
        out_shape=jax.ShapeDtypeStruct((rows, d), F32),
        compiler_params=_params(("parallel",)),
        name="ffn",
    )(h, g, w13, w2, *extra)


def _block_diag(blocks):
    nb, k, r, c = blocks.shape
    eye = jnp.eye(k, dtype=blocks.dtype)
    return jnp.einsum("bkrc,kj->bkrjc", blocks, eye).reshape(nb, k * r, k * c)


def _mixer_in_kernel(h_ref, g_ref, w_ref, cw_ref, cb_ref, lng_ref, lnb_ref,
                     lre_ref, lim_ref, ldt_ref, bre_ref, bim_ref, cre_ref, cim_ref, d_ref,
                     y_ref, hc_ref, q_ref, k_ref, v_ref,
                     win_s, conv_s, are_s, aim_s, bbre_s, bbim_s, hre_s, him_s,
                     *, batch, d_ssm, d_conv, d_attn):
    steps = TOKEN_TILE // batch
    n_gb = d_ssm // (SSM_GROUP_BLOCK * SSM_GROUP)
    chan = SSM_GROUP_BLOCK * SSM_GROUP

    @pl.when(pl.program_id(0) == 0)
    def _first_tile():
        dt = jnp.exp(ldt_ref[...])
        lr = lre_ref[...]
        li = lim_ref[...]
        mag = jnp.exp(lr * dt)
        ab_re = mag * jnp.cos(li * dt)
        ab_im = mag * jnp.sin(li * dt)
        den = lr * lr + li * li
        nr = ab_re - 1.0
        ni = ab_im
        coef_re = (nr * lr + ni * li) / den
        coef_im = (ni * lr - nr * li) / den
        are_s[...] = jnp.broadcast_to(ab_re, are_s.shape)
        aim_s[...] = jnp.broadcast_to(ab_im, aim_s.shape)
        br = bre_ref[...]
        bi = bim_ref[...]
        bbre_s[...] = (coef_re * br - coef_im * bi).astype(BF16)
        bbim_s[...] = (coef_re * bi + coef_im * br).astype(BF16)
        hre_s[...] = jnp.zeros_like(hre_s)
        him_s[...] = jnp.zeros_like(him_s)
        win_s[:TOKEN_TILE, :] = jnp.zeros((TOKEN_TILE, d_conv), F32)

    xb = _rms(h_ref[...], g_ref[...]).astype(BF16)
    c0 = d_ssm
    c1 = c0 + d_conv
    c2 = c1 + d_conv
    mxu_n = 2 * LANES

    def proj(col, width=mxu_n):
        return jnp.dot(xb, w_ref[:, col:col + width], preferred_element_type=F32)

    def conv_in(p):
        cols = slice(p * mxu_n, (p + 1) * mxu_n)
        win_s[TOKEN_TILE:, cols] = proj(c0 + p * mxu_n) * _sigmoid(proj(c1 + p * mxu_n))

    u_parts = {}

    def ssm_in(p):
        u_parts[p] = proj(p * mxu_n)

    def u_of(gb):
        part = u_parts[gb * chan // mxu_n]
        off = gb * chan % mxu_n
        return part[:, off:off + chan]

    def attn_in(piece):
        col = piece * mxu_n
        val = proj(c2 + col)
        ref, off = ((q_ref, k_ref, v_ref)[col // d_attn], col % d_attn)
        if ref is q_ref:
            val = val * (1.0 / math.sqrt(HEAD_DIM))
        ref[:, off:off + mxu_n] = val.astype(BF16)

    bu = {}

    def ssm_drive(gb):
        ub = u_of(gb).astype(BF16)
        bu[gb] = (jnp.dot(ub, bbre_s[gb], preferred_element_type=F32),
                  jnp.dot(ub, bbim_s[gb], preferred_element_type=F32))

    states = {}

    def ssm_out(gb):
        ch = slice(gb * chan, (gb + 1) * chan)
        s_re, s_im = states[gb]
        y = (jnp.dot(s_re.astype(BF16), cre_ref[gb], preferred_element_type=F32)
             - jnp.dot(s_im.astype(BF16), cim_ref[gb], preferred_element_type=F32))
        y = y + d_ref[:, ch] * u_of(gb)
        y_ref[:, ch] = jax.nn.gelu(y).astype(BF16)

    def ssm_scan(gb):
        bu_re, bu_im = bu[gb]
        a_re = are_s[gb]
        a_im = aim_s[gb]
        h_re = hre_s[gb]
        h_im = him_s[gb]
        s_re, s_im = [], []
        for t in range(steps):
            rows = slice(t * batch, (t + 1) * batch)
            h_re, h_im = (a_re * h_re - a_im * h_im + bu_re[rows],
                          a_re * h_im + a_im * h_re + bu_im[rows])
            s_re.append(h_re)
            s_im.append(h_im)
        hre_s[gb] = h_re
        him_s[gb] = h_im
        states[gb] = (jnp.concatenate(s_re, axis=0), jnp.concatenate(s_im, axis=0))

    def conv_piece(cols, tb):
        rows = CONV_BLOCK * batch
        acc = None
        for j in range(CONV_WIDTH):
            r0 = TOKEN_TILE + (tb * CONV_BLOCK - (CONV_WIDTH - 1) + j) * batch
            term = cw_ref[j:j + 1, cols] * win_s[r0:r0 + rows, cols]
            acc = term if acc is None else acc + term
        conv_s[tb * rows:(tb + 1) * rows, cols] = acc

    def conv_quarter(p, part):
        part = p * 4 + part
        cols = slice(0, d_conv)
        n_tb = steps // CONV_BLOCK // 8
        for tb in range(part * n_tb, (part + 1) * n_tb):
            conv_piece(cols, tb)

    mxu = ([(2, functools.partial(conv_in, p)) for p in range(d_conv // mxu_n)]
           + [(1, functools.partial(ssm_in, p)) for p in range(d_ssm // mxu_n)]
           + [(1, functools.partial(ssm_drive, gb)) for gb in range(n_gb)]
           + [(1, functools.partial(attn_in, p)) for p in range(3 * d_attn // mxu_n)])
    convs = [(None, functools.partial(conv_quarter, p, part))
             for p in range(d_conv // mxu_n) for part in range(4)]
    scans = [(gb, functools.partial(ssm_scan, gb)) for gb in range(n_gb)]
    vpu = convs[:4]
    for i in range(max(len(convs) - 4, len(scans))):
        vpu += scans[i:i + 1] + convs[4 + i:5 + i]
    for _ in range(d_conv // mxu_n):
        mxu.pop(0)[1]()
    scanned = []
    for gb, job in vpu:
        units = 0
        while units < 2 and (scanned or mxu):
            if scanned:
                ssm_out(scanned.pop(0))
                units += 1
            else:
                cost, piece = mxu.pop(0)
                piece()
                units += cost
        job()
        if gb is not None:
            scanned.append(gb)
    for gb in scanned:
        ssm_out(gb)
    for _, piece in mxu:
        piece()
    win_s[:TOKEN_TILE, :] = win_s[TOKEN_TILE:, :]

    c = conv_s[...] + cb_ref[...]
    mu = jnp.mean(c, axis=-1, keepdims=True)
    xc = c - mu
    yc = xc * lax.rsqrt(jnp.mean(xc * xc, axis=-1, keepdims=True) + LN_EPS)
    yc = yc * lng_ref[...] + lnb_ref[...]
    hc_ref[...] = (yc * _sigmoid(yc)).astype(BF16)


def _mixer_in(h, g, w, conv_w, conv_b, ln_g, ln_b, lam_re, lam_im, log_dt, b_re, b_im, c_re, c_im,
              d_skip, batch, d_attn):
    rows, d = h.shape
    d_ssm = d_skip.shape[0]
    d_conv = conv_w.shape[1]
    groups = lam_re.shape[0]
    nb = groups // SSM_GROUP_BLOCK
    chan = SSM_GROUP_BLOCK * SSM_GROUP
    width = SSM_GROUP_BLOCK * SSM_STATE
    assert CONV_WIDTH - 1 <= TOKEN_TILE // batch and d_conv % LANES == 0

    lre = lam_re.reshape(nb, 1, width)
    lim = lam_im.reshape(nb, 1, width)
    ldt = jnp.repeat(log_dt, SSM_STATE).reshape(nb, 1, width)
    bre = _block_diag(b_re.reshape(nb, SSM_GROUP_BLOCK, SSM_STATE, SSM_GROUP).transpose(0, 1, 3, 2))
    bim = _block_diag(b_im.reshape(nb, SSM_GROUP_BLOCK, SSM_STATE, SSM_GROUP).transpose(0, 1, 3, 2))
    cre = _block_diag(c_re.reshape(nb, SSM_GROUP_BLOCK, SSM_GROUP, SSM_STATE).transpose(0, 1, 3, 2))
    cim = _block_diag(c_im.reshape(nb, SSM_GROUP_BLOCK, SSM_GROUP, SSM_STATE).transpose(0, 1, 3, 2))

    def row_spec(cols):
        return pl.BlockSpec((TOKEN_TILE, cols), lambda i: (i, 0))

    vec = lambda a: a.reshape(1, -1)
    operands = [g, w, conv_w, vec(conv_b), vec(ln_g), vec(ln_b), lre, lim, ldt, bre, bim,
                cre.astype(BF16), cim.astype(BF16), vec(d_skip)]
    return pl.pallas_call(
        functools.partial(_mixer_in_kernel, batch=batch, d_ssm=d_ssm, d_conv=d_conv, d_attn=d_attn),
        grid=(rows // TOKEN_TILE,),
        in_specs=[row_spec(d)] + [_resident(a.shape) for a in operands],
        out_specs=[row_spec(d_ssm), row_spec(d_conv)] + [row_spec(d_attn)] * 3,
        out_shape=[jax.ShapeDtypeStruct((rows, d_ssm), BF16),
                   jax.ShapeDtypeStruct((rows, d_conv), BF16)]
                  + [jax.ShapeDtypeStruct((rows, d_attn), BF16)] * 3,
        scratch_shapes=[pltpu.VMEM((2 * TOKEN_TILE, d_conv), F32), pltpu.VMEM((TOKEN_TILE, d_conv), F32),
                        pltpu.VMEM((nb, batch, width), F32), pltpu.VMEM((nb, batch, width), F32),
                        pltpu.VMEM((nb, chan, width), BF16), pltpu.VMEM((nb, chan, width), BF16),
                        pltpu.VMEM((nb, batch, width), F32), pltpu.VMEM((nb, batch, width), F32)],
        compiler_params=_params(("arbitrary",)),
        name="mixer_in",
    )(h, *operands)


def _merge_kernel(h_ref, g_ref, wg_ref, y_ref, wglu_ref, hc_ref, wpw_ref, oa_ref, wo_ref,
                  wout_ref, o_ref, *, d):
    x = h_ref[...]
    xb = _rms(x, g_ref[...]).astype(BF16)
    gates = _sigmoid(jnp.dot(xb, wg_ref[...], preferred_element_type=F32))
    sg = jnp.dot(y_ref[...], wglu_ref[...], preferred_element_type=F32)
    o_ssm = sg[:, :d] * _sigmoid(sg[:, d:])
    o_conv = jnp.dot(hc_ref[...], wpw_ref[...], preferred_element_type=F32)
    o_attn = jnp.dot(oa_ref[...], wo_ref[...], preferred_element_type=F32)
    merged = gates[:, :d] * o_ssm + gates[:, d:2 * d] * o_conv + gates[:, 2 * d:] * o_attn
    o_ref[...] = x + jnp.dot(merged.astype(BF16), wout_ref[...], preferred_element_type=F32)


def _merge(h, g, wg, y, wglu, hc, wpw, oa, wo, wout):
    rows, d = h.shape

    def row_spec(width):
        return pl.BlockSpec((TOKEN_TILE, width), lambda i: (i, 0))

    return pl.pallas_call(
        functools.partial(_merge_kernel, d=d),
        grid=(rows // TOKEN_TILE,),
        in_specs=[row_spec(d), _resident((1, d)), _resident(wg.shape),
                  row_spec(y.shape[1]), _resident(wglu.shape),
                  row_spec(hc.shape[1]), _resident(wpw.shape),
                  row_spec(oa.shape[1]), _resident(wo.shape),
                  _resident(wout.shape)],
        out_specs=row_spec(d),
        out_shape=jax.ShapeDtypeStruct((rows, d), F32),
        compiler_params=_params(("parallel",)),
        name="mixer_merge",
    )(h, g, wg, y, wglu, hc, wpw, oa, wo, wout)


def _attn_kernel(qt_ref, k_ref, vt_ref, o_ref):
    blk = TIME_TILE
    qb = ATTN_QUERY_BLOCKS
    tq = qb * blk
    sub = 8
    n_sub = blk // sub
    hpg = ATTN_HEADS_PER_GROUP
    gw = hpg * HEAD_DIM
    cols = hpg * tq
    newest = pl.program_id(1) * qb + qb - 1

    row = lax.broadcasted_iota(jnp.int32, (blk, cols), 0)
    key = (row & (sub - 1)) * n_sub + row // sub
    qry = lax.broadcasted_iota(jnp.int32, (blk, cols), 1) & (tq - 1)
    causal = [key + (qb - 1 - m) * blk < qry for m in range(qb)]
    sublane = lax.broadcasted_iota(jnp.int32, (sub, cols), 0)

    chan_head = lax.broadcasted_iota(jnp.int32, (gw, tq), 0) // HEAD_DIM
    q_groups = []
    for g in range(N_HEADS // hpg):
        qg = qt_ref[0, 0, g * gw:(g + 1) * gw, :]
        zero = jnp.zeros_like(qg)
        q_groups.append(jnp.concatenate(
            [jnp.where(chan_head == hl, qg, zero) for hl in range(hpg)], axis=1))

    def softplus_suffix(sp, run):
        x = sp.reshape(n_sub, sub, cols)
        same_sublane = [None] * n_sub
        tot = jnp.zeros((sub, cols), F32)
        for g in reversed(range(n_sub)):
            same_sublane[g] = tot
            tot = tot + x[g]
        s = tot
        for shift in (1, 2, 4):
            moved = pltpu.roll(s, sub - shift, axis=0)
            s = s + jnp.where(sublane < sub - shift, moved, 0.0)
        higher = s - tot + run
        later = jnp.stack([part + higher for part in same_sublane], axis=0)
        return later.reshape(blk, cols), run + s[0:1, :]

    def scores(j):
        k0 = pl.multiple_of(j * blk, blk)
        return [jnp.dot(k_ref[0, pl.ds(k0, blk), g * gw:(g + 1) * gw], q_groups[g],
                        preferred_element_type=F32) for g in range(N_HEADS // hpg)]

    def block(j, zts, accs, runs, visible=None):
        new_accs, new_runs = [], []
        for g in range(N_HEADS // hpg):
            zt = zts[g]
            sp = jnp.maximum(zt, 0.0) + jnp.log(1.0 + jnp.exp(-jnp.abs(zt)))
            if visible is not None:
                sp = jnp.where(visible, sp, 0.0)
            later, run = softplus_suffix(sp, runs[g])
            w = jnp.exp(zt - sp - later)
            if visible is not None:
                w = jnp.where(visible, w, 0.0)
            w = w.astype(BF16)
            new_runs.append(run)
            for p in range(hpg // 2):
                c0 = 2 * p * tq
                h0 = g * hpg + 2 * p
                res = jnp.dot(vt_ref[0, j, h0 * HEAD_DIM:(h0 + 2) * HEAD_DIM, :], w[:, c0:c0 + 2 * tq],
                              preferred_element_type=F32)
                new_accs.append(accs[h0] + res[:HEAD_DIM, :tq])
                new_accs.append(accs[h0 + 1] + res[HEAD_DIM:, tq:])
        return new_accs, new_runs

    def most_decay(runs):
        m = runs[0]
        for run in runs[1:]:
            m = jnp.minimum(m, run)
        return jnp.min(m)

    accs = [jnp.zeros((HEAD_DIM, tq), F32)] * N_HEADS
    runs = [jnp.zeros((1, cols), F32)] * (N_HEADS // hpg)
    zts = scores(newest)
    for m in range(qb):
        nxt = scores(jnp.maximum(newest - m - 1, 0))
        accs, runs = block(newest - m, zts, accs, runs, causal[m])
        zts = nxt

    def more(carry):
        j, m = carry[:2]
        return jnp.logical_and(j >= 0, m < ATTN_SKIP_DECAY)

    def earlier_pair(carry):
        j, _, z, a, r = carry
        z1 = scores(jnp.maximum(j - 1, 0))
        a, r = block(j, z, a, r)
        z2 = scores(jnp.maximum(j - 2, 0))
        a1, r1 = block(jnp.maximum(j - 1, 0), z1, a, r)
        both = j >= 1
        a = [jnp.where(both, new, old) for new, old in zip(a1, a)]
        r = [jnp.where(both, new, old) for new, old in zip(r1, r)]
        return j - 2, most_decay(r), z2, a, r

    accs = lax.while_loop(more, earlier_pair, (newest - qb, most_decay(runs), zts, accs, runs))[3]
    o_ref[0, 0] = jnp.concatenate(accs, axis=0).astype(BF16)


def _attention(qt, k, vt):
    batch, n_tiles, d_attn, tq = qt.shape
    tile_spec = pl.BlockSpec((1, 1, d_attn, tq), lambda b, i: (b, i, 0, 0))
    return pl.pallas_call(
        _attn_kernel,
        grid=(batch, n_tiles),
        in_specs=[tile_spec,
                  pl.BlockSpec((1,) + k.shape[1:], lambda b, i: (b, 0, 0)),
                  pl.BlockSpec((1,) + vt.shape[1:], lambda b, i: (b, 0, 0, 0))],
        out_specs=tile_spec,
        out_shape=jax.ShapeDtypeStruct(qt.shape, BF16),
        compiler_params=_params(("parallel", "parallel")),
        name="stick_breaking_attention",
    )(qt, k, vt)


def kernel(x, meta_tokens, ffn1_norm, ffn1_w13, ffn1_w2, mix_norm, w_in, ssm_lam_re, ssm_lam_im, ssm_log_dt, ssm_b_re, ssm_b_im, ssm_c_re, ssm_c_im, ssm_d, ssm_w_glu, conv_w, conv_b, conv_ln_g, conv_ln_b, conv_w_out, attn_w_o, w_out, ffn2_norm, ffn2_w13, ffn2_w2, final_norm):
    batch, seq, d = x.shape
    depth = w_in.shape[0]
    d_ssm = ssm_d.shape[1]
    d_conv = conv_w.shape[2]
    d_attn = attn_w_o.shape[1]
    n_mix = d_ssm + 2 * d_conv + 3 * d_attn
    assert d_attn == N_HEADS * HEAD_DIM and batch % 16 == 0 and d_ssm % (SSM_GROUP_BLOCK * SSM_GROUP) == 0
    length = seq + N_META
    padded = -(-length // TIME_TILE) * TIME_TILE
    n_blk = padded // TIME_TILE
    rows = padded * batch
    assert rows % TOKEN_TILE == 0

    meta = jnp.broadcast_to(meta_tokens[:, None, :].astype(x.dtype), (N_META, batch, d))
    h = jnp.pad(jnp.swapaxes(x, 0, 1), ((N_META, padded - length), (0, 0), (0, 0)))
    h = lax.dynamic_update_slice(h, meta, (0, 0, 0)).reshape(rows, d)

    for i in range(depth):
        h = _ffn(h, ffn1_norm[i].reshape(1, d), ffn1_w13[i].astype(BF16), ffn1_w2[i].astype(BF16))

        g_mix = mix_norm[i].reshape(1, d)
        w_in_b = w_in[i].astype(BF16)
        y, hc, q, k, v = _mixer_in(h, g_mix, w_in_b[:, :n_mix], conv_w[i], conv_b[i], conv_ln_g[i],
                                   conv_ln_b[i], ssm_lam_re[i], ssm_lam_im[i], ssm_log_dt[i],
                                   ssm_b_re[i], ssm_b_im[i], ssm_c_re[i], ssm_c_im[i], ssm_d[i],
                                   batch, d_attn)
        n_sub = TIME_TILE // 8
        tq = ATTN_QUERY_BLOCKS * TIME_TILE
        n_tiles = -(-padded // tq)
        n_kb = n_tiles * ATTN_QUERY_BLOCKS
        grow = lambda a: jnp.pad(a.reshape(padded, batch, d_attn),
                                 ((0, n_tiles * tq - padded), (0, 0), (0, 0)))
        keyed = (n_kb, 8, n_sub, batch, d_attn)
        qt = grow(q).reshape(n_tiles, tq, batch, d_attn).transpose(2, 0, 3, 1)
        vt = grow(v).reshape(keyed).transpose(3, 0, 4, 2, 1).reshape(batch, n_kb, d_attn, TIME_TILE)
        kb = grow(k).reshape(keyed).transpose(3, 0, 2, 1, 4).reshape(batch, n_kb * TIME_TILE, d_attn)
        oa = _attention(qt, kb, vt).transpose(1, 3, 0, 2).reshape(n_tiles * tq, batch, d_attn)
        oa = oa[:padded].reshape(rows, d_attn)
        h = _merge(h, g_mix, w_in_b[:, n_mix:], y, ssm_w_glu[i].astype(BF16),
                   hc, conv_w_out[i].astype(BF16), oa, attn_w_o[i].astype(BF16),
                   w_out[i].astype(BF16))

        g_out = final_norm.reshape(1, d) if i == depth - 1 else None
        h = _ffn(h, ffn2_norm[i].reshape(1, d), ffn2_w13[i].astype(BF16), ffn2_w2[i].astype(BF16),
                 g_out)

    return jnp.swapaxes(h.reshape(padded, batch, d)[N_META:length], 0, 1)
```
